```python
import jax, jax.numpy as jnp
from jax import lax
import numpy as np

D_MODEL = 1024
BATCH = 2
SEQ = 8192
DEPTH = 1

SB_HEADS = 8
SB_HEAD_DIM = D_MODEL // SB_HEADS
SB_WIDTH = SB_HEADS * SB_HEAD_DIM
SB_BLOCK = 128
HG_HEADS = 8
HG_KEY_DIM = 128
HG_VAL_DIM = D_MODEL // HG_HEADS
HG_KEY_WIDTH = HG_HEADS * HG_KEY_DIM
HG_VAL_WIDTH = HG_HEADS * HG_VAL_DIM
HG_CHUNK = 64
N_BRANCHES = 2
RMS_EPS = 1e-6
IN_WIDTHS = (SB_WIDTH, SB_WIDTH, SB_WIDTH, SB_WIDTH,
             HG_KEY_WIDTH, HG_KEY_WIDTH, HG_VAL_WIDTH, HG_VAL_WIDTH,
             N_BRANCHES * D_MODEL)
IN_WIDTH = sum(IN_WIDTHS)

kernel_name = "hybrid_stickbreak_hgrn2_gated"


def rmsnorm(x, g):
    xf = x.astype(jnp.float32)
    y = xf * lax.rsqrt(jnp.mean(xf * xf, axis=-1, keepdims=True) + RMS_EPS)
    return (y * g.astype(jnp.float32)).astype(x.dtype)


def sb_attention(q, k, v):
    b, h, s, d = q.shape
    n_blocks = s // SB_BLOCK
    scale = d ** -0.5
    kf = k.astype(jnp.float32)
    vf = v.astype(jnp.float32)
    key_pos = jnp.arange(s)

    def block(i):
        start = i * SB_BLOCK
        qb = lax.dynamic_slice_in_dim(q, start, SB_BLOCK, axis=2).astype(jnp.float32)
        z = jnp.einsum('bhqd,bhkd->bhqk', qb, kf) * scale
        q_pos = start + jnp.arange(SB_BLOCK)
        causal = key_pos[None, :] < q_pos[:, None]
        log_not = jnp.where(causal, jax.nn.log_sigmoid(-z), 0.0)
        survive = lax.cumsum(log_not, axis=3, reverse=True) - log_not
        log_w = jnp.where(causal, jax.nn.log_sigmoid(z) + survive, -jnp.inf)
        w = jnp.exp(log_w)
        return jnp.einsum('bhqk,bhkd->bhqd', w, vf)

    out = lax.map(block, jnp.arange(n_blocks))
    out = jnp.transpose(out, (1, 2, 0, 3, 4)).reshape(b, h, s, d)
    return out.astype(q.dtype)


def hgrn2_chunked(q, k, g, v):
    b, s, h, dk = q.shape
    dv = v.shape[-1]
    n = s // HG_CHUNK

    def to_chunks(t):
        return t.reshape(b, n, HG_CHUNK, h, t.shape[-1]).transpose(1, 0, 3, 2, 4)

    causal = jnp.tril(jnp.ones((HG_CHUNK, HG_CHUNK), dtype=bool))[:, :, None]

    def step(state, inp):
        qc, kc, gc, vc = inp
        cum = jnp.cumsum(gc, axis=2)
        rel = cum[:, :, :, None, :] - cum[:, :, None, :, :]
        decay = jnp.exp(jnp.where(causal, rel, -jnp.inf))
        scores = jnp.einsum('bhtk,bhtsk,bhsk->bhts', qc, decay, kc)
        o = (jnp.einsum('bhts,bhsv->bhtv', scores, vc)
             + jnp.einsum('bhtk,bhkv->bhtv', qc * jnp.exp(cum), state))
        last = cum[:, :, -1:, :]
        state = (jnp.exp(last[:, :, 0, :])[..., None] * state
                 + jnp.einsum('bhsk,bhsv->bhkv', kc * jnp.exp(last - cum), vc))
        return state, o

    state0 = jnp.zeros((b, h, dk, dv), jnp.float32)
    _, o = lax.scan(step, state0, (to_chunks(q), to_chunks(k), to_chunks(g), to_chunks(v)))
    return o.transpose(1, 0, 3, 2, 4).reshape(b, s, h, dv)


def hybrid_layer(x, norm_g, w_in, b_gate, lb, hg_norm_g, w_sb_proj, w_hg_proj, w_out):
    b, s, _ = x.shape
    h = rmsnorm(x, norm_g)
    proj = jnp.einsum('bsd,de->bse', h, w_in)
    split_points = tuple(int(p) for p in np.cumsum(IN_WIDTHS)[:-1])
    sb_q, sb_k, sb_v, sb_z, hg_q, hg_f, hg_i, hg_z, gate_logits = jnp.split(proj, split_points, axis=-1)

    def heads(t):
        return t.reshape(b, s, SB_HEADS, SB_HEAD_DIM).transpose(0, 2, 1, 3)
    sb_o = sb_attention(heads(sb_q), heads(sb_k), heads(sb_v))
    sb_o = sb_o.transpose(0, 2, 1, 3).reshape(b, s, SB_WIDTH)
    u_sb = jnp.einsum('bse,ed->bsd', sb_o * jax.nn.silu(sb_z), w_sb_proj)

    f_logit = hg_f.reshape(b, s, HG_HEADS, HG_KEY_DIM).astype(jnp.float32)
    f = lb + (1.0 - lb) * jax.nn.sigmoid(f_logit)
    g = jnp.log(f)
    kk = 1.0 - f
    qq = jax.nn.silu(hg_q.reshape(b, s, HG_HEADS, HG_KEY_DIM).astype(jnp.float32))
    vv = hg_i.reshape(b, s, HG_HEADS, HG_VAL_DIM).astype(jnp.float32)
    hg_o = hgrn2_chunked(qq, kk, g, vv)
    hg_o = rmsnorm(hg_o, hg_norm_g).reshape(b, s, HG_VAL_WIDTH).astype(x.dtype)
    u_hg = jnp.einsum('bse,ed->bsd', hg_o * jax.nn.silu(hg_z), w_hg_proj)

    gates = jax.nn.sigmoid((gate_logits + b_gate).astype(jnp.float32)).astype(x.dtype)
    gate_sb, gate_hg = jnp.split(gates, N_BRANCHES, axis=-1)
    y = gate_sb * u_sb + gate_hg * u_hg
    return x + jnp.einsum('bsd,de->bse', y, w_out)


def setup_inputs(seed: int = 0) -> dict:
    key = jax.random.key(seed)
    ks = jax.random.split(key, 11)
    f32 = jnp.float32
    x = jax.random.normal(ks[0], (BATCH, SEQ, D_MODEL), f32)
    norm_g = 1.0 + 0.02 * jax.random.normal(ks[1], (DEPTH, D_MODEL), f32)
    w_in = jax.random.normal(ks[2], (DEPTH, D_MODEL, IN_WIDTH), f32) * D_MODEL ** -0.5
    b_gate = 0.1 * jax.random.normal(ks[3], (DEPTH, N_BRANCHES * D_MODEL), f32)
    lb_logits = 0.5 * jax.random.normal(ks[4], (DEPTH + 1, HG_HEADS, HG_KEY_DIM), f32)
    hg_norm_g = 1.0 + 0.02 * jax.random.normal(ks[5], (DEPTH, HG_HEADS, HG_VAL_DIM), f32)
    w_sb_proj = jax.random.normal(ks[6], (DEPTH, SB_WIDTH, D_MODEL), f32) * SB_WIDTH ** -0.5
    w_hg_proj = jax.random.normal(ks[7], (DEPTH, HG_VAL_WIDTH, D_MODEL), f32) * HG_VAL_WIDTH ** -0.5
    w_out = jax.random.normal(ks[8], (DEPTH, D_MODEL, D_MODEL), f32) * D_MODEL ** -0.5
    final_norm_g = 1.0 + 0.02 * jax.random.normal(ks[9], (D_MODEL,), f32)
    return {"x": x, "norm_g": norm_g, "w_in": w_in, "b_gate": b_gate, "lb_logits": lb_logits,
            "hg_norm_g": hg_norm_g, "w_sb_proj": w_sb_proj, "w_hg_proj": w_hg_proj,
            "w_out": w_out, "final_norm_g": final_norm_g}


def reference(x, norm_g, w_in, b_gate, lb_logits, hg_norm_g, w_sb_proj, w_hg_proj, w_out, final_norm_g):
    lb_all = jnp.cumsum(jax.nn.softmax(lb_logits.astype(jnp.float32), axis=0), axis=0)
    for l in range(DEPTH):
        x = hybrid_layer(x, norm_g[l], w_in[l], b_gate[l], lb_all[l], hg_norm_g[l],
                         w_sb_proj[l], w_hg_proj[l], w_out[l])
    return rmsnorm(x, final_norm_g)
```

```python
import functools
import math

import numpy as np
import jax
import jax.numpy as jnp
from jax import lax
from jax.experimental import pallas as pl
from jax.experimental.pallas import tpu as pltpu

F32 = jnp.float32
BF16 = jnp.bfloat16

D_MODEL = 1024
HEADS = 8
HEAD_DIM = 128
RMS_EPS = 1e-6
OFF_SB_Q, OFF_SB_K, OFF_SB_V, OFF_SB_Z = 0, 1024, 2048, 3072
OFF_HG_Q, OFF_HG_F, OFF_HG_I, OFF_HG_Z = 4096, 5120, 6144, 7168
OFF_GATES = 8192
LOG2E = 1.4426950408889634
Q_SCALE = HEAD_DIM ** -0.5 * LOG2E

VMEM_LIMIT = 56 * 1024 * 1024


def _sigmoid(x):
    return 1.0 / (1.0 + jnp.exp(-x))


def _dot(a, b):
    return jnp.dot(a, b, preferred_element_type=F32)


def _dot_nt(a, b):
    return lax.dot_general(a, b, (((1,), (1,)), ((), ())), preferred_element_type=F32)


def _dot_tn(a, b):
    return lax.dot_general(a, b, (((0,), (0,)), ((), ())), preferred_element_type=F32)


def _proj_kernel(x_ref, ng_ref, lbl_ref, wq_ref, wk_ref, wv_ref, whq_ref, whf_ref, whi_ref,
                 q_out, k_out, v_out, hq_out, g_out, hk_out, hv_out, h_scr):
    @pl.when(pl.program_id(1) == 0)
    def _():
        x = x_ref[...]
        ms = jnp.mean(x * x, axis=-1, keepdims=True)
        h_scr[...] = (x * lax.rsqrt(ms + RMS_EPS) * ng_ref[...]).astype(BF16)

    h = h_scr[...]
    q_out[...] = (_dot(h, wq_ref[...]) * Q_SCALE).astype(BF16)
    k_out[...] = _dot(h, wk_ref[...]).astype(BF16)
    v_out[...] = _dot(h, wv_ref[...]).astype(BF16)

    hq = _dot(h, whq_ref[...])
    hq_out[...] = (hq * _sigmoid(hq)).astype(BF16)

    lbl = lbl_ref[...]
    e = jnp.exp(lbl - jnp.max(lbl, axis=0, keepdims=True))
    lb = e[0:1, :] / jnp.sum(e, axis=0, keepdims=True)
    f = lb + (1.0 - lb) * _sigmoid(_dot(h, whf_ref[...]))
    g_out[...] = jnp.log(f)
    hk_out[...] = (1.0 - f).astype(BF16)
    hv_out[...] = _dot(h, whi_ref[...]).astype(BF16)


def _proj_call(x2, norm_g, lb_logits2, w_in_bf, tm, tn):
    m = x2.shape[0]
    grid = (m // tm, D_MODEL // tn)

    def wspec(off):
        return pl.BlockSpec((D_MODEL, tn), lambda i, j, o=off // tn: (0, o + j))

    out_spec = pl.BlockSpec((tm, tn), lambda i, j: (i, j))
    bf = jax.ShapeDtypeStruct((m, D_MODEL), BF16)
    f32 = jax.ShapeDtypeStruct((m, D_MODEL), F32)
    return pl.pallas_call(
        _proj_kernel,
        grid=grid,
        in_specs=[
            pl.BlockSpec((tm, D_MODEL), lambda i, j: (i, 0)),
            pl.BlockSpec((1, D_MODEL), lambda i, j: (0, 0)),
            pl.BlockSpec((lb_logits2.shape[0], tn), lambda i, j: (0, j)),
            wspec(OFF_SB_Q), wspec(OFF_SB_K), wspec(OFF_SB_V),
            wspec(OFF_HG_Q), wspec(OFF_HG_F), wspec(OFF_HG_I),
        ],
        out_specs=[out_spec] * 7,
        out_shape=[bf, bf, bf, bf, f32, bf, bf],
        scratch_shapes=[pltpu.VMEM((tm, D_MODEL), BF16)],
        compiler_params=pltpu.CompilerParams(
            dimension_semantics=("parallel", "arbitrary"), vmem_limit_bytes=VMEM_LIMIT),
        name="proj",
    )(x2, norm_g, lb_logits2, w_in_bf, w_in_bf, w_in_bf, w_in_bf, w_in_bf, w_in_bf)


def _softplus2(z):
    return jnp.maximum(z, 0.0) + jnp.log2(1.0 + jnp.exp2(-jnp.abs(z)))


def _sb_kernel(q_ref, k_ref, v_ref, tri_ref, o_ref, acc_ref, c_ref, *, tq):
    i = pl.program_id(2)
    q = q_ref[...]
    tri = tri_ref[...]

    def block(jb, diagonal):
        start = pl.multiple_of(jb * tq, tq)
        kb = k_ref[pl.ds(start, tq), :]
        vb = v_ref[pl.ds(start, tq), :]
        z = _dot_nt(q, kb)
        sp = _softplus2(z)
        if diagonal:
            row = lax.broadcasted_iota(jnp.int32, (tq, tq), 0)
            col = lax.broadcasted_iota(jnp.int32, (tq, tq), 1)
            causal = col < row
            sp = jnp.where(causal, sp, 0.0)
        sr = _dot(sp.astype(BF16), tri)
        c = c_ref[...]
        c_full = jnp.concatenate([c] * (tq // 128), axis=1)
        w = jnp.exp2(z - sp - sr[:, :tq] - c_full)
        if diagonal:
            w = jnp.where(causal, w, 0.0)
        acc_ref[...] += _dot(w.astype(BF16), vb)
        c_ref[...] = c + sr[:, tq:]

    acc_ref[...] = jnp.zeros_like(acc_ref)
    c_ref[...] = jnp.zeros_like(c_ref)
    block(i, True)

    def body(it, carry):
        block(i - it, False)
        return carry

    lax.fori_loop(1, i + 1, body, 0)
    o_ref[...] = acc_ref[...].astype(o_ref.dtype)


def _sb_tri(tq):
    k = np.arange(tq)[:, None]
    j = np.arange(tq)[None, :]
    tri = np.concatenate([(k > j), np.ones((tq, 128), bool)], axis=1)
    return jnp.asarray(tri, dtype=BF16)


def _sb_call(q, k, v, batch, seq, tq):
    nq = seq // tq
    return pl.pallas_call(
        functools.partial(_sb_kernel, tq=tq),
        grid=(batch, HEADS, nq),
        in_specs=[
            pl.BlockSpec((tq, HEAD_DIM), lambda b, h, i: (b * nq + i, h)),
            pl.BlockSpec((seq, HEAD_DIM), lambda b, h, i: (b, h)),
            pl.BlockSpec((seq, HEAD_DIM), lambda b, h, i: (b, h)),
            pl.BlockSpec((tq, tq + 128), lambda b, h, i: (0, 0)),
        ],
        out_specs=pl.BlockSpec((tq, HEAD_DIM), lambda b, h, i: (b * nq + i, h)),
        out_shape=jax.ShapeDtypeStruct((batch * seq, D_MODEL), BF16),
        scratch_shapes=[pltpu.VMEM((tq, HEAD_DIM), F32), pltpu.VMEM((tq, 128), F32)],
        compiler_params=pltpu.CompilerParams(
            dimension_semantics=("parallel", "parallel", "arbitrary"),
            vmem_limit_bytes=VMEM_LIMIT),
        name="sb_attn",
    )(q, k, v, _sb_tri(tq))


HG_T = 128
HG_LEVELS = 7


def _hg_level_table(t_rows):
    t = np.arange(t_rows)[:, None]
    s = np.arange(t_rows)[None, :]
    x = np.bitwise_xor(t, s)
    lv = np.where(t > s, np.floor(np.log2(np.maximum(x, 1))).astype(np.int32), -2)
    lv = np.where(t == s, -1, lv)
    return jnp.asarray(lv, dtype=jnp.int32)


def _row_cumsum(g):
    rows = g.shape[0]
    row = lax.broadcasted_iota(jnp.int32, g.shape, 0)
    c = g
    d = 1
    while d < rows:
        c = c + jnp.where(row >= d, pltpu.roll(c, d, axis=0), 0.0)
        d *= 2
    return c


def _anchor(cum, level):
    rows, lanes = cum.shape
    half = 1 << level
    group = 2 * half
    if half >= 4:
        parts = [jnp.broadcast_to(cum[s + half - 1:s + half, :], (group, lanes))
                 for s in range(0, rows, group)]
        return jnp.concatenate(parts, axis=0)
    sub = lax.broadcasted_iota(jnp.int32, (8, lanes), 0)
    parts = []
    for s in range(0, rows, 8):
        lo = jnp.broadcast_to(cum[s + 1:s + 2, :], (8, lanes))
        hi = jnp.broadcast_to(cum[s + 5:s + 6, :], (8, lanes))
        parts.append(jnp.where(sub < 4, lo, hi))
    return jnp.concatenate(parts, axis=0)


def _hg_kernel(q_ref, g_ref, k_ref, v_ref, gain_ref, lv_ref, o_ref, state_ref, *, nblk):
    @pl.when(pl.program_id(2) == 0)
    def _():
        state_ref[...] = jnp.zeros_like(state_ref)

    lv = lv_ref[...]
    gain = gain_ref[0]
    row = lax.broadcasted_iota(jnp.int32, (HG_T, HEAD_DIM), 0)

    def body(r, carry):
        rows = pl.ds(pl.multiple_of(r * HG_T, HG_T), HG_T)
        q = q_ref[rows, :].astype(F32)
        k = k_ref[rows, :].astype(F32)
        v = v_ref[rows, :]
        g = g_ref[rows, :]
        cum = _row_cumsum(g)

        scores = jnp.where(lv == -1, _dot_nt(q.astype(BF16), k.astype(BF16)), 0.0)
        for level in range(HG_LEVELS):
            upper = (row & (1 << level)) != 0
            if level == 0:
                d = jnp.where(upper, g, 0.0)
            else:
                d = cum - _anchor(cum, level)
                d = jnp.where(upper, d, -d)
            x = (jnp.where(upper, q, k) * jnp.exp(d)).astype(BF16)
            scores = scores + jnp.where(lv == level, _dot_nt(x, x), 0.0)

        state = state_ref[...]
        total = cum[HG_T - 1:HG_T, :]
        o = _dot(scores.astype(BF16), v)
        o = o + _dot_nt((q * jnp.exp(cum)).astype(BF16), state.astype(BF16))
        k_dec = (k * jnp.exp(total - cum)).astype(BF16)
        state_ref[...] = state * jnp.exp(total) + _dot_tn(v, k_dec)

        ms = jnp.mean(o * o, axis=-1, keepdims=True)
        o_ref[rows, :] = (o * lax.rsqrt(ms + RMS_EPS) * gain).astype(o_ref.dtype)
        return carry

    lax.fori_loop(0, nblk, body, 0)


def _hg_call(hq, g, hk, hv, hg_norm_g3, batch, seq, ts):
    ns = seq // ts
    spec = pl.BlockSpec((ts, HEAD_DIM), lambda b, h, s: (b * ns + s, h))
    return pl.pallas_call(
        functools.partial(_hg_kernel, nblk=ts // HG_T),
        grid=(batch, HEADS, ns),
        in_specs=[
            spec, spec, spec, spec,
            pl.BlockSpec((1, 1, HEAD_DIM), lambda b, h, s: (h, 0, 0)),
            pl.BlockSpec((HG_T, HG_T), lambda b, h, s: (0, 0)),
        ],
        out_specs=spec,
        out_shape=jax.ShapeDtypeStruct((batch * seq, D_MODEL), BF16),
        scratch_shapes=[pltpu.VMEM((HEAD_DIM, HEAD_DIM), F32)],
        compiler_params=pltpu.CompilerParams(
            dimension_semantics=("parallel", "parallel", "arbitrary"),
            vmem_limit_bytes=VMEM_LIMIT),
        name="hgrn2",
    )(hq, g, hk, hv, hg_norm_g3, _hg_level_table(HG_T))


def _out_kernel(x_ref, sbo_ref, hgo_ref, ng_ref, bg_ref, fg_ref,
                wz_sb_ref, wz_hg_ref, wgate_ref, wsb_ref, whg_ref, wout_ref, o_ref):
    x = x_ref[...]
    ms = jnp.mean(x * x, axis=-1, keepdims=True)
    h = (x * lax.rsqrt(ms + RMS_EPS) * ng_ref[...]).astype(BF16)

    sb_z = _dot(h, wz_sb_ref[...])
    a_sb = sbo_ref[...].astype(F32) * (sb_z * _sigmoid(sb_z))
    u_sb = _dot(a_sb.astype(BF16), wsb_ref[...])

    hg_z = _dot(h, wz_hg_ref[...])
    a_hg = hgo_ref[...].astype(F32) * (hg_z * _sigmoid(hg_z))
    u_hg = _dot(a_hg.astype(BF16), whg_ref[...])

    gates = _sigmoid(_dot(h, wgate_ref[...]) + bg_ref[...])
    y = gates[:, :D_MODEL] * u_sb + gates[:, D_MODEL:] * u_hg
    r = x + _dot(y.astype(BF16), wout_ref[...])
    ms2 = jnp.mean(r * r, axis=-1, keepdims=True)
    o_ref[...] = r * lax.rsqrt(ms2 + RMS_EPS) * fg_ref[...]


def _out_call(x2, sb_o, hg_o, norm_g, b_gate, final_g, w_in_bf, w_sb, w_hg, w_out, tm):
    m = x2.shape[0]
    rows = lambda i: (i, 0)
    const = lambda i: (0, 0)

    def wcol(off, width):
        return pl.BlockSpec((D_MODEL, width), lambda i, o=off // width: (0, o))

    return pl.pallas_call(
        _out_kernel,
        grid=(m // tm,),
        in_specs=[
            pl.BlockSpec((tm, D_MODEL), rows),
            pl.BlockSpec((tm, D_MODEL), rows),
            pl.BlockSpec((tm, D_MODEL), rows),
            pl.BlockSpec((1, D_MODEL), const),
            pl.BlockSpec((1, 2 * D_MODEL), const),
            pl.BlockSpec((1, D_MODEL), const),
            wcol(OFF_SB_Z, D_MODEL), wcol(OFF_HG_Z, D_MODEL), wcol(OFF_GATES, 2 * D_MODEL),
            pl.BlockSpec((D_MODEL, D_MODEL), const),
            pl.BlockSpec((D_MODEL, D_MODEL), const),
            pl.BlockSpec((D_MODEL, D_MODEL), const),
        ],
        out_specs=pl.BlockSpec((tm, D_MODEL), rows),
        out_shape=jax.ShapeDtypeStruct((m, D_MODEL), F32),
        compiler_params=pltpu.CompilerParams(
            dimension_semantics=("parallel",), vmem_limit_bytes=VMEM_LIMIT),
        name="out_stage",
    )(x2, sb_o, hg_o, norm_g, b_gate, final_g, w_in_bf, w_in_bf, w_in_bf, w_sb, w_hg, w_out)


def kernel(x, norm_g, w_in, b_gate, lb_logits, hg_norm_g, w_sb_proj, w_hg_proj, w_out, final_norm_g):
    batch, seq, d = x.shape
    assert d == D_MODEL and norm_g.shape[0] == 1, "single-layer block of width 1024"
    x2 = x.reshape(batch * seq, d)
    w_in_bf = w_in[0].astype(BF16)
    lbl2 = lb_logits.reshape(lb_logits.shape[0], HEADS * HEAD_DIM)

    q, k, v, hq, g, hk, hv = _proj_call(x2, norm_g, lbl2, w_in_bf,
                                        tm=min(1024, batch * seq), tn=256)
    sb_o = _sb_call(q, k, v, batch, seq, tq=min(256, seq))
    hg_o = _hg_call(hq, g, hk, hv, hg_norm_g.reshape(HEADS, 1, HEAD_DIM), batch, seq,
                    ts=min(1024, seq))
    out = _out_call(x2, sb_o, hg_o, norm_g, b_gate, final_norm_g.reshape(1, d), w_in_bf,
                    w_sb_proj[0].astype(BF16), w_hg_proj[0].astype(BF16), w_out[0].astype(BF16),
                    tm=256)
    return out.reshape(batch, seq, d)
```

```python
import functools
import math

import numpy as np
import jax
import jax.numpy as jnp
from jax import lax
from jax.experimental import pallas as pl
from jax.experimental.pallas import tpu as pltpu

F32 = jnp.float32
BF16 = jnp.bfloat16

D_MODEL = 1024
HEADS = 8
HEAD_DIM = 128
RMS_EPS = 1e-6
OFF_SB_Q, OFF_SB_K, OFF_SB_V, OFF_SB_Z = 0, 1024, 2048, 3072
OFF_HG_Q, OFF_HG_F, OFF_HG_I, OFF_HG_Z = 4096, 5120, 6144, 7168
OFF_GATES = 8192
LOG2E = 1.4426950408889634
Q_SCALE = HEAD_DIM ** -0.5 * LOG2E

VMEM_LIMIT = 56 * 1024 * 1024


def _sigmoid(x):
    return 1.0 / (1.0 + jnp.exp(-x))


def _dot(a, b):
    return jnp.dot(a, b, preferred_element_type=F32)


def _dot_nt(a, b):
    return lax.dot_general(a, b, (((1,), (1,)), ((), ())), preferred_element_type=F32)


def _dot_tn(a, b):
    return lax.dot_general(a, b, (((0,), (0,)), ((), ())), preferred_element_type=F32)


def _proj_kernel(x_ref, ng_ref, lbl_ref, wq_ref, wk_ref, wv_ref, whq_ref, whf_ref, whi_ref,
                 q_out, k_out, v_out, hq_out, g_out, hk_out, hv_out, h_scr):
    @pl.when(pl.program_id(1) == 0)
    def _():
        x = x_ref[...]
        ms = jnp.mean(x * x, axis=-1, keepdims=True)
        h_scr[...] = (x * lax.rsqrt(ms + RMS_EPS) * ng_ref[...]).astype(BF16)

    h = h_scr[...]
    q_out[...] = (_dot(h, wq_ref[...]) * Q_SCALE).astype(BF16)
    k_out[...] = _dot(h, wk_ref[...]).astype(BF16)
    v_out[...] = _dot(h, wv_ref[...]).astype(BF16)

    hq = _dot(h, whq_ref[...])
    hq_out[...] = (hq * _sigmoid(hq)).astype(BF16)

    lbl = lbl_ref[...]
    e = jnp.exp(lbl - jnp.max(lbl, axis=0, keepdims=True))
    lb = e[0:1, :] / jnp.sum(e, axis=0, keepdims=True)
    f = lb + (1.0 - lb) * _sigmoid(_dot(h, whf_ref[...]))
    g_out[...] = jnp.log(f)
    hk_out[...] = (1.0 - f).astype(BF16)
    hv_out[...] = _dot(h, whi_ref[...]).astype(BF16)


def _proj_call(x2, norm_g, lb_logits2, w_in_bf, tm, tn):
    m = x2.shape[0]
    grid = (m // tm, D_MODEL // tn)

    def wspec(off):
        return pl.BlockSpec((D_MODEL, tn), lambda i, j, o=off // tn: (0, o + j))

    out_spec = pl.BlockSpec((tm, tn), lambda i, j: (i, j))
    bf = jax.ShapeDtypeStruct((m, D_MODEL), BF16)
    f32 = jax.ShapeDtypeStruct((m, D_MODEL), F32)
    return pl.pallas_call(
        _proj_kernel,
        grid=grid,
        in_specs=[
            pl.BlockSpec((tm, D_MODEL), lambda i, j: (i, 0)),
            pl.BlockSpec((1, D_MODEL), lambda i, j: (0, 0)),
            pl.BlockSpec((lb_logits2.shape[0], tn), lambda i, j: (0, j)),
            wspec(OFF_SB_Q), wspec(OFF_SB_K), wspec(OFF_SB_V),
            wspec(OFF_HG_Q), wspec(OFF_HG_F), wspec(OFF_HG_I),
        ],
        out_specs=[out_spec] * 7,
        out_shape=[bf, bf, bf, bf, f32, bf, bf],
        scratch_shapes=[pltpu.VMEM((tm, D_MODEL), BF16)],
        compiler_params=pltpu.CompilerParams(
            dimension_semantics=("parallel", "arbitrary"), vmem_limit_bytes=VMEM_LIMIT),
        name="proj",
    )(x2, norm_g, lb_logits2, w_in_bf, w_in_bf, w_in_bf, w_in_bf, w_in_bf, w_in_bf)


def _neg_abs(z):
    bits = pltpu.bitcast(z, jnp.uint32) | jnp.uint32(0x80000000)
    return pltpu.bitcast(bits, F32)


def _sb_kernel(q_ref, k_ref, v_ref, tri_ref, o_ref, acc_ref, c_ref, *, tq, hps):
    i = pl.program_id(2)
    tri = tri_ref[...]

    def block(jb, diagonal):
        start = pl.multiple_of(jb * tq, tq)
        for hh in range(hps):
            lanes = slice(hh * HEAD_DIM, (hh + 1) * HEAD_DIM)
            q = q_ref[:, lanes]
            kb = k_ref[pl.ds(start, tq), lanes]
            vb = v_ref[pl.ds(start, tq), lanes]
            z = _dot_nt(q, kb)
            l2 = jnp.log2(1.0 + jnp.exp2(_neg_abs(z)))
            sp = jnp.maximum(z, 0.0) + l2
            if diagonal:
                row = lax.broadcasted_iota(jnp.int32, (tq, tq), 0)
                col = lax.broadcasted_iota(jnp.int32, (tq, tq), 1)
                causal = col < row
                sp = jnp.where(causal, sp, 0.0)
            sr = _dot(sp.astype(BF16), tri)
            c = c_ref[:, lanes]
            c_full = jnp.concatenate([c] * (tq // 128), axis=1)
            w = jnp.exp2(jnp.minimum(z, 0.0) - l2 - sr[:, :tq] - c_full)
            if diagonal:
                w = jnp.where(causal, w, 0.0)
            acc_ref[:, lanes] += _dot(w.astype(BF16), vb)
            c_ref[:, lanes] = c + sr[:, tq:]

    acc_ref[...] = jnp.zeros_like(acc_ref)
    c_ref[...] = jnp.zeros_like(c_ref)
    block(i, True)

    def body(it, carry):
        block(i - it, False)
        return carry

    lax.fori_loop(1, i + 1, body, 0)
    o_ref[...] = acc_ref[...].astype(o_ref.dtype)


def _sb_tri(tq):
    k = np.arange(tq)[:, None]
    j = np.arange(tq)[None, :]
    tri = np.concatenate([(k > j), np.ones((tq, 128), bool)], axis=1)
    return jnp.asarray(tri, dtype=BF16)


def _sb_call(q, k, v, batch, seq, tq, hps):
    nq = seq // tq
    width = hps * HEAD_DIM
    return pl.pallas_call(
        functools.partial(_sb_kernel, tq=tq, hps=hps),
        grid=(batch, HEADS // hps, nq),
        in_specs=[
            pl.BlockSpec((tq, width), lambda b, h, i: (b * nq + i, h)),
            pl.BlockSpec((seq, width), lambda b, h, i: (b, h)),
            pl.BlockSpec((seq, width), lambda b, h, i: (b, h)),
            pl.BlockSpec((tq, tq + 128), lambda b, h, i: (0, 0)),
        ],
        out_specs=pl.BlockSpec((tq, width), lambda b, h, i: (b * nq + i, h)),
        out_shape=jax.ShapeDtypeStruct((batch * seq, D_MODEL), BF16),
        scratch_shapes=[pltpu.VMEM((tq, width), F32), pltpu.VMEM((tq, width), F32)],
        compiler_params=pltpu.CompilerParams(
            dimension_semantics=("parallel", "parallel", "arbitrary"),
            vmem_limit_bytes=VMEM_LIMIT),
        name="sb_attn",
    )(q, k, v, _sb_tri(tq))


HG_T = 128
HG_LEVELS = 7


def _hg_level_table(t_rows):
    t = np.arange(t_rows)[:, None]
    s = np.arange(t_rows)[None, :]
    x = np.bitwise_xor(t, s)
    lv = np.where(t > s, np.floor(np.log2(np.maximum(x, 1))).astype(np.int32), -2)
    lv = np.where(t == s, -1, lv)
    return jnp.asarray(lv, dtype=jnp.int32)


def _row_cumsum(g):
    rows = g.shape[0]
    row = lax.broadcasted_iota(jnp.int32, g.shape, 0)
    c = g
    d = 1
    while d < rows:
        c = c + jnp.where(row >= d, pltpu.roll(c, d, axis=0), 0.0)
        d *= 2
    return c


def _anchor(cum, level):
    rows, lanes = cum.shape
    half = 1 << level
    group = 2 * half
    if half >= 4:
        parts = [jnp.broadcast_to(cum[s + half - 1:s + half, :], (group, lanes))
                 for s in range(0, rows, group)]
        return jnp.concatenate(parts, axis=0)
    sub = lax.broadcasted_iota(jnp.int32, (8, lanes), 0)
    parts = []
    for s in range(0, rows, 8):
        lo = jnp.broadcast_to(cum[s + 1:s + 2, :], (8, lanes))
        hi = jnp.broadcast_to(cum[s + 5:s + 6, :], (8, lanes))
        parts.append(jnp.where(sub < 4, lo, hi))
    return jnp.concatenate(parts, axis=0)


def _hg_kernel(q_ref, g_ref, k_ref, v_ref, gain_ref, lv_ref, o_ref, state_ref, *, nblk, hps):
    @pl.when(pl.program_id(2) == 0)
    def _():
        state_ref[...] = jnp.zeros_like(state_ref)

    lv = lv_ref[...]
    row = lax.broadcasted_iota(jnp.int32, (HG_T, HEAD_DIM), 0)

    def body(r, carry):
        rows = pl.ds(pl.multiple_of(r * HG_T, HG_T), HG_T)
        for hh in range(hps):
            lanes = slice(hh * HEAD_DIM, (hh + 1) * HEAD_DIM)
            q = q_ref[rows, lanes].astype(F32)
            k = k_ref[rows, lanes].astype(F32)
            v = v_ref[rows, lanes]
            g = g_ref[rows, lanes]
            cum = _row_cumsum(g)

            scores = jnp.where(lv == -1, _dot_nt(q.astype(BF16), k.astype(BF16)), 0.0)
            for level in range(HG_LEVELS):
                upper = (row & (1 << level)) != 0
                if level == 0:
                    d = jnp.where(upper, g, 0.0)
                else:
                    d = cum - _anchor(cum, level)
                    d = jnp.where(upper, d, -d)
                x = (jnp.where(upper, q, k) * jnp.exp(d)).astype(BF16)
                scores = scores + jnp.where(lv == level, _dot_nt(x, x), 0.0)

            state = state_ref[hh]
            total = cum[HG_T - 1:HG_T, :]
            o = _dot(scores.astype(BF16), v)
            o = o + _dot_nt((q * jnp.exp(cum)).astype(BF16), state.astype(BF16))
            k_dec = (k * jnp.exp(total - cum)).astype(BF16)
            state_ref[hh] = state * jnp.exp(total) + _dot_tn(v, k_dec)

            ms = jnp.mean(o * o, axis=-1, keepdims=True)
            gain = gain_ref[:, lanes]
            o_ref[rows, lanes] = (o * lax.rsqrt(ms + RMS_EPS) * gain).astype(o_ref.dtype)
        return carry

    lax.fori_loop(0, nblk, body, 0)


def _hg_call(hq, g, hk, hv, hg_norm_g2, batch, seq, ts, hps):
    ns = seq // ts
    width = hps * HEAD_DIM
    spec = pl.BlockSpec((ts, width), lambda b, h, s: (b * ns + s, h))
    return pl.pallas_call(
        functools.partial(_hg_kernel, nblk=ts // HG_T, hps=hps),
        grid=(batch, HEADS // hps, ns),
        in_specs=[
            spec, spec, spec, spec,
            pl.BlockSpec((1, width), lambda b, h, s: (0, h)),
            pl.BlockSpec((HG_T, HG_T), lambda b, h, s: (0, 0)),
        ],
        out_specs=spec,
        out_shape=jax.ShapeDtypeStruct((batch * seq, D_MODEL), BF16),
        scratch_shapes=[pltpu.VMEM((hps, HEAD_DIM, HEAD_DIM), F32)],
        compiler_params=pltpu.CompilerParams(
            dimension_semantics=("parallel", "parallel", "arbitrary"),
            vmem_limit_bytes=VMEM_LIMIT),
        name="hgrn2",
    )(hq, g, hk, hv, hg_norm_g2, _hg_level_table(HG_T))


def _out_kernel(x_ref, sbo_ref, hgo_ref, ng_ref, bg_ref, fg_ref,
                wz_sb_ref, wz_hg_ref, wgate_ref, wsb_ref, whg_ref, wout_ref, o_ref):
    x = x_ref[...]
    ms = jnp.mean(x * x, axis=-1, keepdims=True)
    h = (x * lax.rsqrt(ms + RMS_EPS) * ng_ref[...]).astype(BF16)

    sb_z = _dot(h, wz_sb_ref[...])
    a_sb = sbo_ref[...].astype(F32) * (sb_z * _sigmoid(sb_z))
    u_sb = _dot(a_sb.astype(BF16), wsb_ref[...])

    hg_z = _dot(h, wz_hg_ref[...])
    a_hg = hgo_ref[...].astype(F32) * (hg_z * _sigmoid(hg_z))
    u_hg = _dot(a_hg.astype(BF16), whg_ref[...])

    gates = _sigmoid(_dot(h, wgate_ref[...]) + bg_ref[...])
    y = gates[:, :D_MODEL] * u_sb + gates[:, D_MODEL:] * u_hg
    r = x + _dot(y.astype(BF16), wout_ref[...])
    ms2 = jnp.mean(r * r, axis=-1, keepdims=True)
    o_ref[...] = r * lax.rsqrt(ms2 + RMS_EPS) * fg_ref[...]


def _out_call(x2, sb_o, hg_o, norm_g, b_gate, final_g, w_in_bf, w_sb, w_hg, w_out, tm):
    m = x2.shape[0]
    rows = lambda i: (i, 0)
    const = lambda i: (0, 0)

    def wcol(off, width):
        return pl.BlockSpec((D_MODEL, width), lambda i, o=off // width: (0, o))

    return pl.pallas_call(
        _out_kernel,
        grid=(m // tm,),
        in_specs=[
            pl.BlockSpec((tm, D_MODEL), rows),
            pl.BlockSpec((tm, D_MODEL), rows),
            pl.BlockSpec((tm, D_MODEL), rows),
            pl.BlockSpec((1, D_MODEL), const),
            pl.BlockSpec((1, 2 * D_MODEL), const),
            pl.BlockSpec((1, D_MODEL), const),
            wcol(OFF_SB_Z, D_MODEL), wcol(OFF_HG_Z, D_MODEL), wcol(OFF_GATES, 2 * D_MODEL),
            pl.BlockSpec((D_MODEL, D_MODEL), const),
            pl.BlockSpec((D_MODEL, D_MODEL), const),
            pl.BlockSpec((D_MODEL, D_MODEL), const),
        ],
        out_specs=pl.BlockSpec((tm, D_MODEL), rows),
        out_shape=jax.ShapeDtypeStruct((m, D_MODEL), F32),
        compiler_params=pltpu.CompilerParams(
            dimension_semantics=("parallel",), vmem_limit_bytes=VMEM_LIMIT),
        name="out_stage",
    )(x2, sb_o, hg_o, norm_g, b_gate, final_g, w_in_bf, w_in_bf, w_in_bf, w_sb, w_hg, w_out)


def kernel(x, norm_g, w_in, b_gate, lb_logits, hg_norm_g, w_sb_proj, w_hg_proj, w_out, final_norm_g):
    batch, seq, d = x.shape
    assert d == D_MODEL and norm_g.shape[0] == 1, "single-layer block of width 1024"
    x2 = x.reshape(batch * seq, d)
    w_in_bf = w_in[0].astype(BF16)
    lbl2 = lb_logits.reshape(lb_logits.shape[0], HEADS * HEAD_DIM)

    q, k, v, hq, g, hk, hv = _proj_call(x2, norm_g, lbl2, w_in_bf,
                                        tm=min(1024, batch * seq), tn=256)
    sb_o = _sb_call(q, k, v, batch, seq, tq=min(256, seq), hps=4)
    hg_o = _hg_call(hq, g, hk, hv, hg_norm_g.reshape(1, HEADS * HEAD_DIM), batch, seq,
                    ts=min(1024, seq), hps=4)
    out = _out_call(x2, sb_o, hg_o, norm_g, b_gate, final_norm_g.reshape(1, d), w_in_bf,
                    w_sb_proj[0].astype(BF16), w_hg_proj[0].astype(BF16), w_out[0].astype(BF16),
                    tm=256)
    return out.reshape(batch, seq, d)
```

```python
import functools
import math

import numpy as np
import jax
import jax.numpy as jnp
from jax import lax
from jax.experimental import pallas as pl
from jax.experimental.pallas import tpu as pltpu

F32 = jnp.float32
BF16 = jnp.bfloat16

D_MODEL = 1024
HEADS = 8
HEAD_DIM = 128
RMS_EPS = 1e-6
OFF_SB_Q, OFF_SB_K, OFF_SB_V, OFF_SB_Z = 0, 1024, 2048, 3072
OFF_HG_Q, OFF_HG_F, OFF_HG_I, OFF_HG_Z = 4096, 5120, 6144, 7168
OFF_GATES = 8192
LOG2E = 1.4426950408889634
Q_SCALE = HEAD_DIM ** -0.5 * LOG2E

VMEM_LIMIT = 56 * 1024 * 1024


def _sigmoid(x):
    return 1.0 / (1.0 + jnp.exp(-x))


def _dot(a, b):
    return jnp.dot(a, b, preferred_element_type=F32)


def _dot_nt(a, b):
    return lax.dot_general(a, b, (((1,), (1,)), ((), ())), preferred_element_type=F32)


def _dot_tn(a, b):
    return lax.dot_general(a, b, (((0,), (0,)), ((), ())), preferred_element_type=F32)


def _proj_kernel(x_ref, ng_ref, lbl_ref, wq_ref, wk_ref, wv_ref, whq_ref, whf_ref, whi_ref,
                 q_out, k_out, v_out, hq_out, g_out, hk_out, hv_out, h_scr):
    @pl.when(pl.program_id(1) == 0)
    def _():
        x = x_ref[...]
        ms = jnp.mean(x * x, axis=-1, keepdims=True)
        h_scr[...] = (x * lax.rsqrt(ms + RMS_EPS) * ng_ref[...]).astype(BF16)

    h = h_scr[...]
    q_out[...] = (_dot(h, wq_ref[...]) * Q_SCALE).astype(BF16)
    k_out[...] = _dot(h, wk_ref[...]).astype(BF16)
    v_out[...] = _dot(h, wv_ref[...]).astype(BF16)

    hq = _dot(h, whq_ref[...])
    hq_out[...] = (hq * _sigmoid(hq)).astype(BF16)

    lbl = lbl_ref[...]
    e = jnp.exp(lbl - jnp.max(lbl, axis=0, keepdims=True))
    lb = e[0:1, :] / jnp.sum(e, axis=0, keepdims=True)
    f = lb + (1.0 - lb) * _sigmoid(_dot(h, whf_ref[...]))
    g_out[...] = jnp.log(f)
    hk_out[...] = (1.0 - f).astype(BF16)
    hv_out[...] = _dot(h, whi_ref[...]).astype(BF16)


def _proj_call(x2, norm_g, lb_logits2, w_in_bf, tm, tn):
    m = x2.shape[0]
    grid = (m // tm, D_MODEL // tn)

    def wspec(off):
        return pl.BlockSpec((D_MODEL, tn), lambda i, j, o=off // tn: (0, o + j))

    out_spec = pl.BlockSpec((tm, tn), lambda i, j: (i, j))
    bf = jax.ShapeDtypeStruct((m, D_MODEL), BF16)
    f32 = jax.ShapeDtypeStruct((m, D_MODEL), F32)
    return pl.pallas_call(
        _proj_kernel,
        grid=grid,
        in_specs=[
            pl.BlockSpec((tm, D_MODEL), lambda i, j: (i, 0)),
            pl.BlockSpec((1, D_MODEL), lambda i, j: (0, 0)),
            pl.BlockSpec((lb_logits2.shape[0], tn), lambda i, j: (0, j)),
            wspec(OFF_SB_Q), wspec(OFF_SB_K), wspec(OFF_SB_V),
            wspec(OFF_HG_Q), wspec(OFF_HG_F), wspec(OFF_HG_I),
        ],
        out_specs=[out_spec] * 7,
        out_shape=[bf, bf, bf, bf, f32, bf, bf],
        scratch_shapes=[pltpu.VMEM((tm, D_MODEL), BF16)],
        compiler_params=pltpu.CompilerParams(
            dimension_semantics=("parallel", "arbitrary"), vmem_limit_bytes=VMEM_LIMIT),
        name="proj",
    )(x2, norm_g, lb_logits2, w_in_bf, w_in_bf, w_in_bf, w_in_bf, w_in_bf, w_in_bf)


SB_DEAD = 160.0
SB_PAST_DEAD = 1e30


def _sb_kernel(q_ref, k_ref, v_ref, tri_ref, o_ref, acc_ref, c_ref, *, tq, hps):
    i = pl.program_id(2)
    tri = tri_ref[...]

    def block(jb, diagonal):
        start = pl.multiple_of(jb * tq, tq)
        for hh in range(hps):
            lanes = slice(hh * HEAD_DIM, (hh + 1) * HEAD_DIM)
            q = q_ref[:, lanes]
            kb = k_ref[pl.ds(start, tq), lanes]
            vb = v_ref[pl.ds(start, tq), lanes]
            z = _dot_nt(q, kb)
            neg = jnp.minimum(z, 0.0)
            pos = jnp.maximum(z, 0.0)
            l2 = jnp.log2(1.0 + jnp.exp2(neg - pos))
            sp = pos + l2
            if diagonal:
                row = lax.broadcasted_iota(jnp.int32, (tq, tq), 0)
                col = lax.broadcasted_iota(jnp.int32, (tq, tq), 1)
                causal = col < row
                sp = jnp.where(causal, sp, 0.0)
            sr = _dot(sp.astype(BF16), tri)
            c = c_ref[:, lanes]
            c_full = jnp.concatenate([c] * (tq // 128), axis=1)
            w = jnp.exp2(neg - l2 - sr[:, :tq] - c_full)
            if diagonal:
                w = jnp.where(causal, w, 0.0)
            acc_ref[:, lanes] += _dot(w.astype(BF16), vb)
            c_ref[:, lanes] = c + sr[:, tq:]

    def alive():
        return (jnp.min(c_ref[...]) < SB_DEAD).astype(jnp.int32)

    acc_ref[...] = jnp.zeros_like(acc_ref)
    c_ref[...] = jnp.zeros_like(c_ref)
    block(i, True)
    c_ref[...] += jnp.where(i == 0, SB_PAST_DEAD, 0.0)
    block(jnp.maximum(i - 1, 0), False)

    def cond(carry):
        jb, live = carry
        return jnp.logical_and(jb >= 0, live != 0)

    def body(carry):
        jb, _ = carry
        block(jb, False)
        return jb - 1, alive()

    lax.while_loop(cond, body, (i - 2, alive()))
    o_ref[...] = acc_ref[...].astype(o_ref.dtype)


def _sb_tri(tq):
    k = np.arange(tq)[:, None]
    j = np.arange(tq)[None, :]
    tri = np.concatenate([(k > j), np.ones((tq, 128), bool)], axis=1)
    return jnp.asarray(tri, dtype=BF16)


def _sb_call(q, k, v, batch, seq, tq, hps):
    nq = seq // tq
    width = hps * HEAD_DIM
    return pl.pallas_call(
        functools.partial(_sb_kernel, tq=tq, hps=hps),
        grid=(batch, HEADS // hps, nq),
        in_specs=[
            pl.BlockSpec((tq, width), lambda b, h, i: (b * nq + i, h)),
            pl.BlockSpec((seq, width), lambda b, h, i: (b, h)),
            pl.BlockSpec((seq, width), lambda b, h, i: (b, h)),
            pl.BlockSpec((tq, tq + 128), lambda b, h, i: (0, 0)),
        ],
        out_specs=pl.BlockSpec((tq, width), lambda b, h, i: (b * nq + i, h)),
        out_shape=jax.ShapeDtypeStruct((batch * seq, D_MODEL), BF16),
        scratch_shapes=[pltpu.VMEM((tq, width), F32), pltpu.VMEM((tq, width), F32)],
        compiler_params=pltpu.CompilerParams(
            dimension_semantics=("parallel", "parallel", "arbitrary"),
            vmem_limit_bytes=VMEM_LIMIT),
        name="sb_attn",
    )(q, k, v, _sb_tri(tq))


HG_T = 128
HG_LEVELS = 7


def _hg_level_table(t_rows):
    t = np.arange(t_rows)[:, None]
    s = np.arange(t_rows)[None, :]
    x = np.bitwise_xor(t, s)
    lv = np.where(t > s, np.floor(np.log2(np.maximum(x, 1))).astype(np.int32), -2)
    lv = np.where(t == s, -1, lv)
    return jnp.asarray(lv, dtype=jnp.int32)


def _row_cumsum(g):
    rows = g.shape[0]
    row = lax.broadcasted_iota(jnp.int32, g.shape, 0)
    c = g
    d = 1
    while d < rows:
        c = c + jnp.where(row >= d, pltpu.roll(c, d, axis=0), 0.0)
        d *= 2
    return c


def _anchor(cum, level):
    rows, lanes = cum.shape
    half = 1 << level
    group = 2 * half
    if half >= 4:
        parts = [jnp.broadcast_to(cum[s + half - 1:s + half, :], (group, lanes))
                 for s in range(0, rows, group)]
        return jnp.concatenate(parts, axis=0)
    sub = lax.broadcasted_iota(jnp.int32, (8, lanes), 0)
    parts = []
    for s in range(0, rows, 8):
        lo = jnp.broadcast_to(cum[s + 1:s + 2, :], (8, lanes))
        hi = jnp.broadcast_to(cum[s + 5:s + 6, :], (8, lanes))
        parts.append(jnp.where(sub < 4, lo, hi))
    return jnp.concatenate(parts, axis=0)


def _hg_kernel(q_ref, g_ref, k_ref, v_ref, gain_ref, lv_ref, o_ref, state_ref, *, nblk, hps):
    @pl.when(pl.program_id(2) == 0)
    def _():
        state_ref[...] = jnp.zeros_like(state_ref)

    lv = lv_ref[...]
    row = lax.broadcasted_iota(jnp.int32, (HG_T, HEAD_DIM), 0)

    def body(r, carry):
        rows = pl.ds(pl.multiple_of(r * HG_T, HG_T), HG_T)
        for hh in range(hps):
            lanes = slice(hh * HEAD_DIM, (hh + 1) * HEAD_DIM)
            q = q_ref[rows, lanes].astype(F32)
            k = k_ref[rows, lanes].astype(F32)
            v = v_ref[rows, lanes]
            g = g_ref[rows, lanes]
            cum = _row_cumsum(g)

            scores = jnp.where(lv == -1, _dot_nt(q.astype(BF16), k.astype(BF16)), 0.0)
            for level in range(HG_LEVELS):
                upper = (row & (1 << level)) != 0
                if level == 0:
                    d = jnp.where(upper, g, 0.0)
                else:
                    d = cum - _anchor(cum, level)
                    d = jnp.where(upper, d, -d)
                x = (jnp.where(upper, q, k) * jnp.exp(d)).astype(BF16)
                scores = scores + jnp.where(lv == level, _dot_nt(x, x), 0.0)

            state = state_ref[hh]
            total = cum[HG_T - 1:HG_T, :]
            o = _dot(scores.astype(BF16), v)
            o = o + _dot_nt((q * jnp.exp(cum)).astype(BF16), state.astype(BF16))
            k_dec = (k * jnp.exp(total - cum)).astype(BF16)
            state_ref[hh] = state * jnp.exp(total) + _dot_tn(v, k_dec)

            ms = jnp.mean(o * o, axis=-1, keepdims=True)
            gain = gain_ref[:, lanes]
            o_ref[rows, lanes] = (o * lax.rsqrt(ms + RMS_EPS) * gain).astype(o_ref.dtype)
        return carry

    lax.fori_loop(0, nblk, body, 0)


def _hg_call(hq, g, hk, hv, hg_norm_g2, batch, seq, ts, hps):
    ns = seq // ts
    width = hps * HEAD_DIM
    spec = pl.BlockSpec((ts, width), lambda b, h, s: (b * ns + s, h))
    return pl.pallas_call(
        functools.partial(_hg_kernel, nblk=ts // HG_T, hps=hps),
        grid=(batch, HEADS // hps, ns),
        in_specs=[
            spec, spec, spec, spec,
            pl.BlockSpec((1, width), lambda b, h, s: (0, h)),
            pl.BlockSpec((HG_T, HG_T), lambda b, h, s: (0, 0)),
        ],
        out_specs=spec,
        out_shape=jax.ShapeDtypeStruct((batch * seq, D_MODEL), BF16),
        scratch_shapes=[pltpu.VMEM((hps, HEAD_DIM, HEAD_DIM), F32)],
        compiler_params=pltpu.CompilerParams(
            dimension_semantics=("parallel", "parallel", "arbitrary"),
            vmem_limit_bytes=VMEM_LIMIT),
        name="hgrn2",
    )(hq, g, hk, hv, hg_norm_g2, _hg_level_table(HG_T))


def _out_kernel(x_ref, sbo_ref, hgo_ref, ng_ref, bg_ref, fg_ref,
                wz_sb_ref, wz_hg_ref, wgate_ref, wsb_ref, whg_ref, wout_ref, o_ref):
    x = x_ref[...]
    ms = jnp.mean(x * x, axis=-1, keepdims=True)
    h = (x * lax.rsqrt(ms + RMS_EPS) * ng_ref[...]).astype(BF16)

    sb_z = _dot(h, wz_sb_ref[...])
    a_sb = sbo_ref[...].astype(F32) * (sb_z * _sigmoid(sb_z))
    u_sb = _dot(a_sb.astype(BF16), wsb_ref[...])

    hg_z = _dot(h, wz_hg_ref[...])
    a_hg = hgo_ref[...].astype(F32) * (hg_z * _sigmoid(hg_z))
    u_hg = _dot(a_hg.astype(BF16), whg_ref[...])

    gates = _sigmoid(_dot(h, wgate_ref[...]) + bg_ref[...])
    y = gates[:, :D_MODEL] * u_sb + gates[:, D_MODEL:] * u_hg
    r = x + _dot(y.astype(BF16), wout_ref[...])
    ms2 = jnp.mean(r * r, axis=-1, keepdims=True)
    o_ref[...] = r * lax.rsqrt(ms2 + RMS_EPS) * fg_ref[...]


def _out_call(x2, sb_o, hg_o, norm_g, b_gate, final_g, w_in_bf, w_sb, w_hg, w_out, tm):
    m = x2.shape[0]
    rows = lambda i: (i, 0)
    const = lambda i: (0, 0)

    def wcol(off, width):
        return pl.BlockSpec((D_MODEL, width), lambda i, o=off // width: (0, o))

    return pl.pallas_call(
        _out_kernel,
        grid=(m // tm,),
        in_specs=[
            pl.BlockSpec((tm, D_MODEL), rows),
            pl.BlockSpec((tm, D_MODEL), rows),
            pl.BlockSpec((tm, D_MODEL), rows),
            pl.BlockSpec((1, D_MODEL), const),
            pl.BlockSpec((1, 2 * D_MODEL), const),
            pl.BlockSpec((1, D_MODEL), const),
            wcol(OFF_SB_Z, D_MODEL), wcol(OFF_HG_Z, D_MODEL), wcol(OFF_GATES, 2 * D_MODEL),
            pl.BlockSpec((D_MODEL, D_MODEL), const),
            pl.BlockSpec((D_MODEL, D_MODEL), const),
            pl.BlockSpec((D_MODEL, D_MODEL), const),
        ],
        out_specs=pl.BlockSpec((tm, D_MODEL), rows),
        out_shape=jax.ShapeDtypeStruct((m, D_MODEL), F32),
        compiler_params=pltpu.CompilerParams(
            dimension_semantics=("parallel",), vmem_limit_bytes=VMEM_LIMIT),
        name="out_stage",
    )(x2, sb_o, hg_o, norm_g, b_gate, final_g, w_in_bf, w_in_bf, w_in_bf, w_sb, w_hg, w_out)


def kernel(x, norm_g, w_in, b_gate, lb_logits, hg_norm_g, w_sb_proj, w_hg_proj, w_out, final_norm_g):
    batch, seq, d = x.shape
    assert d == D_MODEL and norm_g.shape[0] == 1, "single-layer block of width 1024"
    x2 = x.reshape(batch * seq, d)
    w_in_bf = w_in[0].astype(BF16)
    lbl2 = lb_logits.reshape(lb_logits.shape[0], HEADS * HEAD_DIM)

    q, k, v, hq, g, hk, hv = _proj_call(x2, norm_g, lbl2, w_in_bf,
                                        tm=min(1024, batch * seq), tn=256)
    sb_o = _sb_call(q, k, v, batch, seq, tq=min(256, seq), hps=4)
    hg_o = _hg_call(hq, g, hk, hv, hg_norm_g.reshape(1, HEADS * HEAD_DIM), batch, seq,
                    ts=min(1024, seq), hps=4)
    out = _out_call(x2, sb_o, hg_o, norm_g, b_gate, final_norm_g.reshape(1, d), w_in_bf,
                    w_sb_proj[0].astype(BF16), w_hg_proj[0].astype(BF16), w_out[0].astype(BF16),
                    tm=256)
    return out.reshape(batch, seq, d)
```

```python
import functools

import numpy as np
import jax
import jax.numpy as jnp
from jax import lax
from jax.experimental import pallas as pl
from jax.experimental.pallas import tpu as pltpu

F32 = jnp.float32
BF16 = jnp.bfloat16

D_MODEL = 1024
HEADS = 8
HEAD_DIM = 128
RMS_EPS = 1e-6
OFF_SB_Q, OFF_SB_K, OFF_SB_V, OFF_SB_Z = 0, 1024, 2048, 3072
OFF_HG_Q, OFF_HG_F, OFF_HG_I, OFF_HG_Z = 4096, 5120, 6144, 7168
OFF_GATES = 8192
LOG2E = 1.4426950408889634
Q_SCALE = HEAD_DIM ** -0.5 * LOG2E

VMEM_LIMIT = 56 * 1024 * 1024


def _sigmoid(x):
    return 1.0 / (1.0 + jnp.exp(-x))


def _dot(a, b):
    return jnp.dot(a, b, preferred_element_type=F32)


def _dot_nt(a, b):
    return lax.dot_general(a, b, (((1,), (1,)), ((), ())), preferred_element_type=F32)


def _dot_tn(a, b):
    return lax.dot_general(a, b, (((0,), (0,)), ((), ())), preferred_element_type=F32)


def _proj_kernel(x_ref, ng_ref, lbl_ref, wq_ref, wk_ref, wv_ref, whq_ref, whf_ref, whi_ref,
                 q_out, k_out, v_out, hq_out, ghi_out, glo_out, hk_out, hv_out, h_scr):
    @pl.when(pl.program_id(1) == 0)
    def _():
        x = x_ref[...]
        ms = jnp.mean(x * x, axis=-1, keepdims=True)
        h_scr[...] = (x * lax.rsqrt(ms + RMS_EPS) * ng_ref[...]).astype(BF16)

    h = h_scr[...]
    q_out[...] = (_dot(h, wq_ref[...]) * Q_SCALE).astype(BF16)
    k_out[...] = _dot(h, wk_ref[...]).astype(BF16)
    v_out[...] = _dot(h, wv_ref[...]).astype(BF16)

    hq = _dot(h, whq_ref[...])
    hq_out[...] = (hq * _sigmoid(hq)).astype(BF16)

    lbl = lbl_ref[...]
    e = jnp.exp(lbl - jnp.max(lbl, axis=0, keepdims=True))
    lb = e[0:1, :] / jnp.sum(e, axis=0, keepdims=True)
    f = lb + (1.0 - lb) * _sigmoid(_dot(h, whf_ref[...]))
    g2 = jnp.log(f) * LOG2E
    g_hi = g2.astype(BF16)
    ghi_out[...] = g_hi
    glo_out[...] = (g2 - g_hi.astype(F32)).astype(BF16)
    hk_out[...] = (1.0 - f).astype(BF16)
    hv_out[...] = _dot(h, whi_ref[...]).astype(BF16)


def _proj_call(x2, norm_g, lb_logits2, w_in_bf, tm, tn):
    m = x2.shape[0]
    grid = (m // tm, D_MODEL // tn)

    def wspec(off):
        return pl.BlockSpec((D_MODEL, tn), lambda i, j, o=off // tn: (0, o + j))

    out_spec = pl.BlockSpec((tm, tn), lambda i, j: (i, j))
    bf = jax.ShapeDtypeStruct((m, D_MODEL), BF16)
    return pl.pallas_call(
        _proj_kernel,
        grid=grid,
        in_specs=[
            pl.BlockSpec((tm, D_MODEL), lambda i, j: (i, 0)),
            pl.BlockSpec((1, D_MODEL), lambda i, j: (0, 0)),
            pl.BlockSpec((lb_logits2.shape[0], tn), lambda i, j: (0, j)),
            wspec(OFF_SB_Q), wspec(OFF_SB_K), wspec(OFF_SB_V),
            wspec(OFF_HG_Q), wspec(OFF_HG_F), wspec(OFF_HG_I),
        ],
        out_specs=[out_spec] * 8,
        out_shape=[bf] * 8,
        scratch_shapes=[pltpu.VMEM((tm, D_MODEL), BF16)],
        compiler_params=pltpu.CompilerParams(
            dimension_semantics=("parallel", "arbitrary"), vmem_limit_bytes=VMEM_LIMIT),
        name="proj",
    )(x2, norm_g, lb_logits2, w_in_bf, w_in_bf, w_in_bf, w_in_bf, w_in_bf, w_in_bf)


SB_DEAD = 160.0
SB_PAST_DEAD = 1e30


def _sb_kernel(q_ref, k_ref, v_ref, tri_ref, o_ref, acc_ref, c_ref, *, tq, hps):
    i = pl.program_id(2)
    tri = tri_ref[...]
    head_lanes = [slice(hh * HEAD_DIM, (hh + 1) * HEAD_DIM) for hh in range(hps)]

    def visit(blocks):
        tiles = [(jb, kind, hh) for jb, kind in blocks for hh in range(hps)]
        row = lax.broadcasted_iota(jnp.int32, (tq, tq), 0)
        col = lax.broadcasted_iota(jnp.int32, (tq, tq), 1)
        causal = col < row

        zs = []
        for jb, kind, hh in tiles:
            kb = k_ref[pl.ds(pl.multiple_of(jb * tq, tq), tq), head_lanes[hh]]
            zs.append(_dot_nt(q_ref[:, head_lanes[hh]], kb))

        parts = []
        for (jb, kind, hh), z in zip(tiles, zs):
            neg = jnp.minimum(z, 0.0)
            pos = jnp.maximum(z, 0.0)
            l2 = jnp.log2(1.0 + jnp.exp2(neg - pos))
            sp = pos + l2
            if kind == "diag":
                sp = jnp.where(causal, sp, 0.0)
            sr = _dot(sp.astype(BF16), tri)
            parts.append((neg - l2, sr))

        carry = [c_ref[:, lanes] for lanes in head_lanes]
        acc = [acc_ref[:, lanes] for lanes in head_lanes]
        for (jb, kind, hh), (log_beta, sr) in zip(tiles, parts):
            c = carry[hh]
            if kind == "left":
                c = c + jnp.where(i == 0, SB_PAST_DEAD, 0.0)
            c_full = jnp.concatenate([c] * (tq // 128), axis=1)
            w = jnp.exp2(log_beta - sr[:, :tq] - c_full)
            if kind == "diag":
                w = jnp.where(causal, w, 0.0)
            vb = v_ref[pl.ds(pl.multiple_of(jb * tq, tq), tq), head_lanes[hh]]
            acc[hh] = acc[hh] + _dot(w.astype(BF16), vb)
            carry[hh] = c + sr[:, tq:]
        for hh, lanes in enumerate(head_lanes):
            c_ref[:, lanes] = carry[hh]
            acc_ref[:, lanes] = acc[hh]

    def alive():
        return (jnp.min(c_ref[...]) < SB_DEAD).astype(jnp.int32)

    acc_ref[...] = jnp.zeros_like(acc_ref)
    c_ref[...] = jnp.zeros_like(c_ref)
    visit([(i, "diag"), (jnp.maximum(i - 1, 0), "left")])

    def cond(carry):
        jb, live = carry
        return jnp.logical_and(jb >= 0, live != 0)

    def body(carry):
        jb, _ = carry
        visit([(jb, "plain")])
        return jb - 1, alive()

    lax.while_loop(cond, body, (i - 2, alive()))
    o_ref[...] = acc_ref[...].astype(o_ref.dtype)


def _sb_tri(tq):
    k = np.arange(tq)[:, None]
    j = np.arange(tq)[None, :]
    tri = np.concatenate([(k > j), np.ones((tq, 128), bool)], axis=1)
    return jnp.asarray(tri, dtype=BF16)


def _sb_call(q, k, v, batch, seq, tq, hps):
    nq = seq // tq
    width = hps * HEAD_DIM
    return pl.pallas_call(
        functools.partial(_sb_kernel, tq=tq, hps=hps),
        grid=(batch, HEADS // hps, nq),
        in_specs=[
            pl.BlockSpec((tq, width), lambda b, h, i: (b * nq + i, h)),
            pl.BlockSpec((seq, width), lambda b, h, i: (b, h)),
            pl.BlockSpec((seq, width), lambda b, h, i: (b, h)),
            pl.BlockSpec((tq, tq + 128), lambda b, h, i: (0, 0)),
        ],
        out_specs=pl.BlockSpec((tq, width), lambda b, h, i: (b * nq + i, h)),
        out_shape=jax.ShapeDtypeStruct((batch * seq, D_MODEL), BF16),
        scratch_shapes=[pltpu.VMEM((tq, width), F32), pltpu.VMEM((tq, width), F32)],
        compiler_params=pltpu.CompilerParams(
            dimension_semantics=("parallel", "parallel", "arbitrary"),
            vmem_limit_bytes=VMEM_LIMIT),
        name="sb_attn",
    )(q, k, v, _sb_tri(tq))


HG_T = 128
HG_HALF = HG_T // 2
HG_SMALL_LEVELS = (0, 1, 2)
HG_BIG_LEVELS = (3, 4, 5, 6)


def _hg_constants():
    t = np.arange(HG_T)[:, None]
    s = np.arange(HG_T)[None, :]
    top = np.floor(np.log2(np.maximum(np.bitwise_xor(t, s), 1))).astype(np.int64)
    small = [t == s] + [(t > s) & (top == lv) for lv in HG_SMALL_LEVELS]
    u = np.arange(HG_HALF)
    big = [(u[:, None] >> lv) == (u[None, :] >> lv) for lv in HG_BIG_LEVELS]
    r = np.arange(HG_T)
    sel = [(r & 1).astype(np.float32)] + [np.where((r >> lv) & 1, 1.0, -1.0) for lv in (1, 2)]
    sel = np.broadcast_to(np.stack(sel)[:, :, None], (3, HG_T, HEAD_DIM))
    return (jnp.asarray(t >= s, dtype=BF16),
            jnp.asarray(np.stack(small), dtype=F32),
            jnp.asarray(np.stack(big), dtype=F32),
            jnp.asarray(sel, dtype=F32))


def _take_rows(a, starts, size):
    parts = [a[s:s + size, :] for s in starts]
    return parts[0] if len(parts) == 1 else jnp.concatenate(parts, axis=0)


def _small_anchor(cum, level):
    rows, lanes = cum.shape
    parts = []
    if level == 2:
        for s in range(0, rows, 8):
            parts.append(jnp.broadcast_to(cum[s + 3:s + 4, :], (8, lanes)))
    else:
        sub = lax.broadcasted_iota(jnp.int32, (8, lanes), 0)
        for s in range(0, rows, 8):
            lo = jnp.broadcast_to(cum[s + 1:s + 2, :], (8, lanes))
            hi = jnp.broadcast_to(cum[s + 5:s + 6, :], (8, lanes))
            parts.append(jnp.where(sub < 4, lo, hi))
    return jnp.concatenate(parts, axis=0)


def _hg_kernel(q_ref, ghi_ref, glo_ref, k_ref, v_ref, gain_ref, tri_ref, smask_ref, bmask_ref,
               sel_ref, o_ref, state_ref, *, nblk, hps):
    @pl.when(pl.program_id(2) == 0)
    def _():
        state_ref[...] = jnp.zeros_like(state_ref)

    tri = tri_ref[...]
    heads = range(hps)

    def body(r, carry):
        rows = pl.ds(pl.multiple_of(r * HG_T, HG_T), HG_T)
        q_bf, k_bf, v_bf, g_hi, g_lo = [
            [ref[rows, hh * HEAD_DIM:(hh + 1) * HEAD_DIM] for hh in heads]
            for ref in (q_ref, k_ref, v_ref, ghi_ref, glo_ref)]

        cum2 = [_dot(tri, jnp.concatenate([g_hi[h], g_lo[h]], axis=1)) for h in heads]
        cum = [c[:, :HEAD_DIM] + c[:, HEAD_DIM:] for c in cum2]
        q = [a.astype(F32) for a in q_bf]
        k = [a.astype(F32) for a in k_bf]
        v = [a.astype(F32) for a in v_bf]

        prods = [[(_dot_nt(q_bf[h], k_bf[h]) * smask_ref[0]).astype(BF16)] for h in heads]
        for idx, level in enumerate(HG_SMALL_LEVELS):
            ps = []
            for h in heads:
                if level == 0:
                    d = g_hi[h].astype(F32) + g_lo[h].astype(F32)
                else:
                    d = cum[h] - _small_anchor(cum[h], level)
                e = jnp.exp2(d * sel_ref[idx])
                ps.append(_dot_nt((q[h] * e).astype(BF16), (k[h] * e).astype(BF16)))
            for h in heads:
                prods[h].append((ps[h] * smask_ref[idx + 1]).astype(BF16))

        big = [[] for _ in heads]
        for idx, level in enumerate(HG_BIG_LEVELS):
            half = 1 << level
            starts = list(range(0, HG_T, 2 * half))
            upper = [s + half for s in starts]
            ps = []
            for h in heads:
                anchor = jnp.concatenate(
                    [jnp.broadcast_to(cum[h][s + half - 1:s + half, :], (half, HEAD_DIM))
                     for s in starts], axis=0)
                xu = _take_rows(q[h], upper, half) * jnp.exp2(_take_rows(cum[h], upper, half) - anchor)
                xl = _take_rows(k[h], starts, half) * jnp.exp2(anchor - _take_rows(cum[h], starts, half))
                ps.append(_dot_nt(xu.astype(BF16), xl.astype(BF16)))
            for h in heads:
                big[h].append((ps[h] * bmask_ref[idx]).astype(BF16))

        n_small = 1 + len(HG_SMALL_LEVELS)
        outs, new_states = [], []
        for h in heads:
            state = state_ref[h]
            total = cum[h][HG_T - 1:HG_T, :]
            o = _dot(jnp.concatenate(prods[h], axis=1), jnp.concatenate([v_bf[h]] * n_small, axis=0))
            o = o + _dot_nt((q[h] * jnp.exp2(cum[h])).astype(BF16), state.astype(BF16))
            k_dec = (k[h] * jnp.exp2(total - cum[h])).astype(BF16)
            new_states.append(state * jnp.exp2(total) + _dot_tn(v_bf[h], k_dec))
            outs.append(o)
        contribs = [[_dot(big[h][idx], _take_rows(v[h], range(0, HG_T, 2 << level), 1 << level).astype(BF16))
                     for idx, level in enumerate(HG_BIG_LEVELS)] for h in heads]

        for h in heads:
            state_ref[h] = new_states[h]
            o_tiles = [outs[h][s:s + 8, :] for s in range(0, HG_T, 8)]
            for idx, level in enumerate(HG_BIG_LEVELS):
                half = 1 << level
                tile_ids = [(s + half + r) // 8 for s in range(0, HG_T, 2 * half) for r in range(0, half, 8)]
                for n, tid in enumerate(tile_ids):
                    o_tiles[tid] = o_tiles[tid] + contribs[h][idx][8 * n:8 * n + 8, :]
            o = jnp.concatenate(o_tiles, axis=0)
            ms = jnp.mean(o * o, axis=-1, keepdims=True)
            lanes = slice(h * HEAD_DIM, (h + 1) * HEAD_DIM)
            o_ref[rows, lanes] = (o * lax.rsqrt(ms + RMS_EPS) * gain_ref[:, lanes]).astype(o_ref.dtype)
        return carry

    lax.fori_loop(0, nblk, body, 0)


def _hg_call(hq, g_hi, g_lo, hk, hv, hg_norm_g2, batch, seq, ts, hps):
    ns = seq // ts
    width = hps * HEAD_DIM
    spec = pl.BlockSpec((ts, width), lambda b, h, s: (b * ns + s, h))
    tri, small_masks, big_masks, sel = _hg_constants()

    def whole(a):
        return pl.BlockSpec(a.shape, lambda b, h, s, nd=a.ndim: (0,) * nd)

    return pl.pallas_call(
        functools.partial(_hg_kernel, nblk=ts // HG_T, hps=hps),
        grid=(batch, HEADS // hps, ns),
        in_specs=[
            spec, spec, spec, spec, spec,
            pl.BlockSpec((1, width), lambda b, h, s: (0, h)),
            whole(tri), whole(small_masks), whole(big_masks), whole(sel),
        ],
        out_specs=spec,
        out_shape=jax.ShapeDtypeStruct((batch * seq, D_MODEL), BF16),
        scratch_shapes=[pltpu.VMEM((hps, HEAD_DIM, HEAD_DIM), F32)],
        compiler_params=pltpu.CompilerParams(
            dimension_semantics=("parallel", "parallel", "arbitrary"),
            vmem_limit_bytes=VMEM_LIMIT),
        name="hgrn2",
    )(hq, g_hi, g_lo, hk, hv, hg_norm_g2, tri, small_masks, big_masks, sel)


def _out_kernel(x_ref, sbo_ref, hgo_ref, ng_ref, bg_ref, fg_ref,
                wz_sb_ref, wz_hg_ref, wgate_ref, wsb_ref, whg_ref, wout_ref, o_ref):
    x = x_ref[...]
    ms = jnp.mean(x * x, axis=-1, keepdims=True)
    h = (x * lax.rsqrt(ms + RMS_EPS) * ng_ref[...]).astype(BF16)

    sb_z = _dot(h, wz_sb_ref[...])
    a_sb = sbo_ref[...].astype(F32) * (sb_z * _sigmoid(sb_z))
    u_sb = _dot(a_sb.astype(BF16), wsb_ref[...])

    hg_z = _dot(h, wz_hg_ref[...])
    a_hg = hgo_ref[...].astype(F32) * (hg_z * _sigmoid(hg_z))
    u_hg = _dot(a_hg.astype(BF16), whg_ref[...])

    gates = _sigmoid(_dot(h, wgate_ref[...]) + bg_ref[...])
    y = gates[:, :D_MODEL] * u_sb + gates[:, D_MODEL:] * u_hg
    r = x + _dot(y.astype(BF16), wout_ref[...])
    ms2 = jnp.mean(r * r, axis=-1, keepdims=True)
    o_ref[...] = r * lax.rsqrt(ms2 + RMS_EPS) * fg_ref[...]


def _out_call(x2, sb_o, hg_o, norm_g, b_gate, final_g, w_in_bf, w_sb, w_hg, w_out, tm):
    m = x2.shape[0]
    rows = lambda i: (i, 0)
    const = lambda i: (0, 0)

    def wcol(off, width):
        return pl.BlockSpec((D_MODEL, width), lambda i, o=off // width: (0, o))

    return pl.pallas_call(
        _out_kernel,
        grid=(m // tm,),
        in_specs=[
            pl.BlockSpec((tm, D_MODEL), rows),
            pl.BlockSpec((tm, D_MODEL), rows),
            pl.BlockSpec((tm, D_MODEL), rows),
            pl.BlockSpec((1, D_MODEL), const),
            pl.BlockSpec((1, 2 * D_MODEL), const),
            pl.BlockSpec((1, D_MODEL), const),
            wcol(OFF_SB_Z, D_MODEL), wcol(OFF_HG_Z, D_MODEL), wcol(OFF_GATES, 2 * D_MODEL),
            pl.BlockSpec((D_MODEL, D_MODEL), const),
            pl.BlockSpec((D_MODEL, D_MODEL), const),
            pl.BlockSpec((D_MODEL, D_MODEL), const),
        ],
        out_specs=pl.BlockSpec((tm, D_MODEL), rows),
        out_shape=jax.ShapeDtypeStruct((m, D_MODEL), F32),
        compiler_params=pltpu.CompilerParams(
            dimension_semantics=("parallel",), vmem_limit_bytes=VMEM_LIMIT),
        name="out_stage",
    )(x2, sb_o, hg_o, norm_g, b_gate, final_g, w_in_bf, w_in_bf, w_in_bf, w_sb, w_hg, w_out)


def kernel(x, norm_g, w_in, b_gate, lb_logits, hg_norm_g, w_sb_proj, w_hg_proj, w_out, final_norm_g):
    batch, seq, d = x.shape
    assert d == D_MODEL and norm_g.shape[0] == 1, "single-layer block of width 1024"
    x2 = x.reshape(batch * seq, d)
    w_in_bf = w_in[0].astype(BF16)
    lbl2 = lb_logits.reshape(lb_logits.shape[0], HEADS * HEAD_DIM)

    q, k, v, hq, g_hi, g_lo, hk, hv = _proj_call(x2, norm_g, lbl2, w_in_bf,
                                                 tm=min(1024, batch * seq), tn=256)
    sb_o = _sb_call(q, k, v, batch, seq, tq=min(256, seq), hps=4)
    hg_o = _hg_call(hq, g_hi, g_lo, hk, hv, hg_norm_g.reshape(1, HEADS * HEAD_DIM), batch, seq,
                    ts=min(1024, seq), hps=4)
    out = _out_call(x2, sb_o, hg_o, norm_g, b_gate, final_norm_g.reshape(1, d), w_in_bf,
                    w_sb_proj[0].astype(BF16), w_hg_proj[0].astype(BF16), w_out[0].astype(BF16),
                    tm=256)
    return out.reshape(batch, seq, d)
```

```python
import functools

import numpy as np
import jax
import jax.numpy as jnp
from jax import lax
from jax.experimental import pallas as pl
from jax.experimental.pallas import tpu as pltpu

F32 = jnp.float32
BF16 = jnp.bfloat16

D_MODEL = 1024
HEADS = 8
HEAD_DIM = 128
RMS_EPS = 1e-6
OFF_SB_Q, OFF_SB_K, OFF_SB_V, OFF_SB_Z = 0, 1024, 2048, 3072
OFF_HG_Q, OFF_HG_F, OFF_HG_I, OFF_HG_Z = 4096, 5120, 6144, 7168
OFF_GATES = 8192
LOG2E = 1.4426950408889634
Q_SCALE = HEAD_DIM ** -0.5 * LOG2E

VMEM_LIMIT = 56 * 1024 * 1024


def _sigmoid(x):
    return 1.0 / (1.0 + jnp.exp(-x))


def _dot(a, b):
    return jnp.dot(a, b, preferred_element_type=F32)


def _dot_nt(a, b):
    return lax.dot_general(a, b, (((1,), (1,)), ((), ())), preferred_element_type=F32)


def _dot_tn(a, b):
    return lax.dot_general(a, b, (((0,), (0,)), ((), ())), preferred_element_type=F32)


def _proj_kernel(x_ref, ng_ref, lbl_ref, wq_ref, wk_ref, wv_ref, whq_ref, whf_ref, whi_ref,
                 q_out, k_out, v_out, hq_out, ghi_out, glo_out, hk_out, hv_out, gsum_out, h_scr):
    @pl.when(pl.program_id(1) == 0)
    def _():
        x = x_ref[...]
        ms = jnp.mean(x * x, axis=-1, keepdims=True)
        h_scr[...] = (x * lax.rsqrt(ms + RMS_EPS) * ng_ref[...]).astype(BF16)

    h = h_scr[...]
    q_out[...] = (_dot(h, wq_ref[...]) * Q_SCALE).astype(BF16)
    k_out[...] = _dot(h, wk_ref[...]).astype(BF16)
    v_out[...] = _dot(h, wv_ref[...]).astype(BF16)

    hq = _dot(h, whq_ref[...])
    hq_out[...] = (hq * _sigmoid(hq)).astype(BF16)

    lbl = lbl_ref[...]
    e = jnp.exp(lbl - jnp.max(lbl, axis=0, keepdims=True))
    lb = e[0:1, :] / jnp.sum(e, axis=0, keepdims=True)
    f = lb + (1.0 - lb) * _sigmoid(_dot(h, whf_ref[...]))
    g2 = jnp.log(f) * LOG2E
    g_hi = g2.astype(BF16)
    ghi_out[...] = g_hi
    glo_out[...] = (g2 - g_hi.astype(F32)).astype(BF16)
    gsum_out[...] = jnp.sum(g2.reshape(g2.shape[0] // HG_TM, HG_TM, g2.shape[1]), axis=1)
    hk_out[...] = (1.0 - f).astype(BF16)
    hv_out[...] = _dot(h, whi_ref[...]).astype(BF16)


def _proj_call(x2, norm_g, lb_logits2, w_in_bf, tm, tn):
    m = x2.shape[0]
    grid = (m // tm, D_MODEL // tn)

    def wspec(off):
        return pl.BlockSpec((D_MODEL, tn), lambda i, j, o=off // tn: (0, o + j))

    out_spec = pl.BlockSpec((tm, tn), lambda i, j: (i, j))
    bf = jax.ShapeDtypeStruct((m, D_MODEL), BF16)
    return pl.pallas_call(
        _proj_kernel,
        grid=grid,
        in_specs=[
            pl.BlockSpec((tm, D_MODEL), lambda i, j: (i, 0)),
            pl.BlockSpec((1, D_MODEL), lambda i, j: (0, 0)),
            pl.BlockSpec((lb_logits2.shape[0], tn), lambda i, j: (0, j)),
            wspec(OFF_SB_Q), wspec(OFF_SB_K), wspec(OFF_SB_V),
            wspec(OFF_HG_Q), wspec(OFF_HG_F), wspec(OFF_HG_I),
        ],
        out_specs=[out_spec] * 8 + [pl.BlockSpec((tm // HG_TM, tn), lambda i, j: (i, j))],
        out_shape=[bf] * 8 + [jax.ShapeDtypeStruct((m // HG_TM, D_MODEL), F32)],
        scratch_shapes=[pltpu.VMEM((tm, D_MODEL), BF16)],
        compiler_params=pltpu.CompilerParams(
            dimension_semantics=("parallel", "arbitrary"), vmem_limit_bytes=VMEM_LIMIT),
        name="proj",
    )(x2, norm_g, lb_logits2, w_in_bf, w_in_bf, w_in_bf, w_in_bf, w_in_bf, w_in_bf)


SB_DEAD = 160.0
SB_PAST_DEAD = 1e30


def _sb_kernel(q_ref, k_ref, v_ref, tri_ref, o_ref, acc_ref, c_ref, *, tq, hps):
    i = pl.program_id(2)
    tri = tri_ref[...]
    head_lanes = [slice(hh * HEAD_DIM, (hh + 1) * HEAD_DIM) for hh in range(hps)]

    def visit(blocks):
        tiles = [(jb, kind, hh) for jb, kind in blocks for hh in range(hps)]
        row = lax.broadcasted_iota(jnp.int32, (tq, tq), 0)
        col = lax.broadcasted_iota(jnp.int32, (tq, tq), 1)
        causal = col < row

        zs = []
        for jb, kind, hh in tiles:
            kb = k_ref[pl.ds(pl.multiple_of(jb * tq, tq), tq), head_lanes[hh]]
            zs.append(_dot_nt(q_ref[:, head_lanes[hh]], kb))

        parts = []
        for (jb, kind, hh), z in zip(tiles, zs):
            neg = jnp.minimum(z, 0.0)
            pos = jnp.maximum(z, 0.0)
            l2 = jnp.log2(1.0 + jnp.exp2(neg - pos))
            sp = pos + l2
            if kind == "diag":
                sp = jnp.where(causal, sp, 0.0)
            sr = _dot(sp.astype(BF16), tri)
            parts.append((neg - l2, sr))

        carry = [c_ref[:, lanes] for lanes in head_lanes]
        acc = [acc_ref[:, lanes] for lanes in head_lanes]
        for (jb, kind, hh), (log_beta, sr) in zip(tiles, parts):
            c = carry[hh]
            if kind == "left":
                c = c + jnp.where(i == 0, SB_PAST_DEAD, 0.0)
            c_full = jnp.concatenate([c] * (tq // 128), axis=1)
            w = jnp.exp2(log_beta - sr[:, :tq] - c_full)
            if kind == "diag":
                w = jnp.where(causal, w, 0.0)
            vb = v_ref[pl.ds(pl.multiple_of(jb * tq, tq), tq), head_lanes[hh]]
            acc[hh] = acc[hh] + _dot(w.astype(BF16), vb)
            carry[hh] = c + sr[:, tq:]
        for hh, lanes in enumerate(head_lanes):
            c_ref[:, lanes] = carry[hh]
            acc_ref[:, lanes] = acc[hh]

    def alive():
        return (jnp.min(c_ref[...]) < SB_DEAD).astype(jnp.int32)

    acc_ref[...] = jnp.zeros_like(acc_ref)
    c_ref[...] = jnp.zeros_like(c_ref)
    visit([(i, "diag"), (jnp.maximum(i - 1, 0), "left")])

    def cond(carry):
        jb, live = carry
        return jnp.logical_and(jb >= 0, live != 0)

    def body(carry):
        jb, _ = carry
        visit([(jb, "plain")])
        return jb - 1, alive()

    lax.while_loop(cond, body, (i - 2, alive()))
    o_ref[...] = acc_ref[...].astype(o_ref.dtype)


def _sb_tri(tq):
    k = np.arange(tq)[:, None]
    j = np.arange(tq)[None, :]
    tri = np.concatenate([(k > j), np.ones((tq, 128), bool)], axis=1)
    return jnp.asarray(tri, dtype=BF16)


def _sb_call(q, k, v, batch, seq, tq, hps):
    nq = seq // tq
    width = hps * HEAD_DIM
    return pl.pallas_call(
        functools.partial(_sb_kernel, tq=tq, hps=hps),
        grid=(batch, HEADS // hps, nq),
        in_specs=[
            pl.BlockSpec((tq, width), lambda b, h, i: (b * nq + i, h)),
            pl.BlockSpec((seq, width), lambda b, h, i: (b, h)),
            pl.BlockSpec((seq, width), lambda b, h, i: (b, h)),
            pl.BlockSpec((tq, tq + 128), lambda b, h, i: (0, 0)),
        ],
        out_specs=pl.BlockSpec((tq, width), lambda b, h, i: (b * nq + i, h)),
        out_shape=jax.ShapeDtypeStruct((batch * seq, D_MODEL), BF16),
        scratch_shapes=[pltpu.VMEM((tq, width), F32), pltpu.VMEM((tq, width), F32)],
        compiler_params=pltpu.CompilerParams(
            dimension_semantics=("parallel", "parallel", "arbitrary"),
            vmem_limit_bytes=VMEM_LIMIT),
        name="sb_attn",
    )(q, k, v, _sb_tri(tq))


HG_T = 128
HG_HALF = HG_T // 2
HG_SMALL_LEVELS = (0, 1, 2)
HG_BIG_LEVELS = (3, 4, 5, 6)
HG_TM = 64
HG_MILD_BITS = 100.0
HG_MILD_UNROLL = 8


def _hg_constants():
    t = np.arange(HG_T)[:, None]
    s = np.arange(HG_T)[None, :]
    top = np.floor(np.log2(np.maximum(np.bitwise_xor(t, s), 1))).astype(np.int64)
    small = [t == s] + [(t > s) & (top == lv) for lv in HG_SMALL_LEVELS]
    u = np.arange(HG_HALF)
    big = [(u[:, None] >> lv) == (u[None, :] >> lv) for lv in HG_BIG_LEVELS]
    r = np.arange(HG_T)
    sel = [(r & 1).astype(np.float32)] + [np.where((r >> lv) & 1, 1.0, -1.0) for lv in (1, 2)]
    sel = np.broadcast_to(np.stack(sel)[:, :, None], (3, HG_T, HEAD_DIM))
    return (jnp.asarray(t >= s, dtype=BF16),
            jnp.asarray(t >= s, dtype=F32),
            jnp.asarray(np.stack(small), dtype=F32),
            jnp.asarray(np.stack(big), dtype=F32),
            jnp.asarray(sel, dtype=F32))


def _take_rows(a, starts, size):
    parts = [a[s:s + size, :] for s in starts]
    return parts[0] if len(parts) == 1 else jnp.concatenate(parts, axis=0)


def _small_anchor(cum, level):
    rows, lanes = cum.shape
    parts = []
    if level == 2:
        for s in range(0, rows, 8):
            parts.append(jnp.broadcast_to(cum[s + 3:s + 4, :], (8, lanes)))
    else:
        sub = lax.broadcasted_iota(jnp.int32, (8, lanes), 0)
        for s in range(0, rows, 8):
            lo = jnp.broadcast_to(cum[s + 1:s + 2, :], (8, lanes))
            hi = jnp.broadcast_to(cum[s + 5:s + 6, :], (8, lanes))
            parts.append(jnp.where(sub < 4, lo, hi))
    return jnp.concatenate(parts, axis=0)


def _hg_kernel(q_ref, ghi_ref, glo_ref, k_ref, v_ref, gsum_ref, gain_ref, tri_ref, cmask_ref,
               smask_ref, bmask_ref, sel_ref, o_ref, state_ref, *, nblk, hps):
    @pl.when(pl.program_id(2) == 0)
    def _():
        state_ref[...] = jnp.zeros_like(state_ref)

    tri = tri_ref[...]
    heads = range(hps)
    width = hps * HEAD_DIM

    def load_block(r, t_rows):
        rows = pl.ds(pl.multiple_of(r * t_rows, t_rows), t_rows)
        q_bf, k_bf, v_bf, g_hi, g_lo = [
            [ref[rows, hh * HEAD_DIM:(hh + 1) * HEAD_DIM] for hh in heads]
            for ref in (q_ref, k_ref, v_ref, ghi_ref, glo_ref)]
        tri_t = tri[:t_rows, :t_rows]
        cum2 = [_dot(tri_t, jnp.concatenate([g_hi[h], g_lo[h]], axis=1)) for h in heads]
        cum = [c[:, :HEAD_DIM] + c[:, HEAD_DIM:] for c in cum2]
        return rows, q_bf, k_bf, v_bf, g_hi, g_lo, cum

    def store_block(rows, h, o):
        ms = jnp.mean(o * o, axis=-1, keepdims=True)
        lanes = slice(h * HEAD_DIM, (h + 1) * HEAD_DIM)
        o_ref[rows, lanes] = (o * lax.rsqrt(ms + RMS_EPS) * gain_ref[:, lanes]).astype(o_ref.dtype)

    def mild_body(r, carry):
        blocks = [load_block(r * HG_MILD_UNROLL + j, HG_TM) for j in range(HG_MILD_UNROLL)]
        items = [(j, h) for j in range(HG_MILD_UNROLL) for h in heads]
        causal = cmask_ref[:HG_TM, :HG_TM] > 0.0
        qt_bf, kt, decay = {}, {}, {}
        for j, h in items:
            _, q_bf, k_bf, _, _, _, cum = blocks[j]
            qt_bf[j, h] = (q_bf[h].astype(F32) * jnp.exp2(cum[h])).astype(BF16)
            kt[j, h] = k_bf[h].astype(F32) * jnp.exp2(-cum[h])
            decay[j, h] = jnp.exp2(cum[h][HG_TM - 1:HG_TM, :])
        ps = {it: _dot_nt(qt_bf[it], kt[it].astype(BF16)) for it in items}
        kv = {(j, h): _dot_tn(blocks[j][3][h], (kt[j, h] * decay[j, h]).astype(BF16)) for j, h in items}
        o_intra = {(j, h): _dot(jnp.where(causal, ps[j, h], 0.0).astype(BF16), blocks[j][3][h])
                   for j, h in items}
        states = {}
        for h in heads:
            state = state_ref[h]
            for j in range(HG_MILD_UNROLL):
                states[j, h] = state
                state = state * decay[j, h] + kv[j, h]
            state_ref[h] = state
        o_inter = {it: _dot_nt(qt_bf[it], states[it].astype(BF16)) for it in items}
        for j, h in items:
            store_block(blocks[j][0], h, o_intra[j, h] + o_inter[j, h])
        return carry

    def body(r, carry):
        rows, q_bf, k_bf, v_bf, g_hi, g_lo, cum = load_block(r, HG_T)
        q = [a.astype(F32) for a in q_bf]
        k = [a.astype(F32) for a in k_bf]
        v = [a.astype(F32) for a in v_bf]

        prods = [[(_dot_nt(q_bf[h], k_bf[h]) * smask_ref[0]).astype(BF16)] for h in heads]
        for idx, level in enumerate(HG_SMALL_LEVELS):
            ps = []
            for h in heads:
                if level == 0:
                    d = g_hi[h].astype(F32) + g_lo[h].astype(F32)
                else:
                    d = cum[h] - _small_anchor(cum[h], level)
                e = jnp.exp2(d * sel_ref[idx])
                ps.append(_dot_nt((q[h] * e).astype(BF16), (k[h] * e).astype(BF16)))
            for h in heads:
                prods[h].append((ps[h] * smask_ref[idx + 1]).astype(BF16))

        big = [[] for _ in heads]
        for idx, level in enumerate(HG_BIG_LEVELS):
            half = 1 << level
            starts = list(range(0, HG_T, 2 * half))
            upper = [s + half for s in starts]
            ps = []
            for h in heads:
                anchor = jnp.concatenate(
                    [jnp.broadcast_to(cum[h][s + half - 1:s + half, :], (half, HEAD_DIM))
                     for s in starts], axis=0)
                xu = _take_rows(q[h], upper, half) * jnp.exp2(_take_rows(cum[h], upper, half) - anchor)
                xl = _take_rows(k[h], starts, half) * jnp.exp2(anchor - _take_rows(cum[h], starts, half))
                ps.append(_dot_nt(xu.astype(BF16), xl.astype(BF16)))
            for h in heads:
                big[h].append((ps[h] * bmask_ref[idx]).astype(BF16))

        n_small = 1 + len(HG_SMALL_LEVELS)
        outs, new_states = [], []
        for h in heads:
            state = state_ref[h]
            total = cum[h][HG_T - 1:HG_T, :]
            o = _dot(jnp.concatenate(prods[h], axis=1), jnp.concatenate([v_bf[h]] * n_small, axis=0))
            o = o + _dot_nt((q[h] * jnp.exp2(cum[h])).astype(BF16), state.astype(BF16))
            k_dec = (k[h] * jnp.exp2(total - cum[h])).astype(BF16)
            new_states.append(state * jnp.exp2(total) + _dot_tn(v_bf[h], k_dec))
            outs.append(o)
        contribs = [[_dot(big[h][idx], _take_rows(v[h], range(0, HG_T, 2 << level), 1 << level).astype(BF16))
                     for idx, level in enumerate(HG_BIG_LEVELS)] for h in heads]

        for h in heads:
            state_ref[h] = new_states[h]
            o_tiles = [outs[h][s:s + 8, :] for s in range(0, HG_T, 8)]
            for idx, level in enumerate(HG_BIG_LEVELS):
                half = 1 << level
                tile_ids = [(s + half + r) // 8 for s in range(0, HG_T, 2 * half) for r in range(0, half, 8)]
                for n, tid in enumerate(tile_ids):
                    o_tiles[tid] = o_tiles[tid] + contribs[h][idx][8 * n:8 * n + 8, :]
            store_block(rows, h, jnp.concatenate(o_tiles, axis=0))
        return carry

    mild = jnp.min(gsum_ref[...]) >= -HG_MILD_BITS

    @pl.when(mild)
    def _():
        lax.fori_loop(0, nblk * HG_T // (HG_TM * HG_MILD_UNROLL), mild_body, 0)

    @pl.when(jnp.logical_not(mild))
    def _():
        lax.fori_loop(0, nblk, body, 0)


def _hg_call(hq, g_hi, g_lo, hk, hv, g_sum, hg_norm_g2, batch, seq, ts, hps):
    ns = seq // ts
    width = hps * HEAD_DIM
    spec = pl.BlockSpec((ts, width), lambda b, h, s: (b * ns + s, h))
    sum_spec = pl.BlockSpec((ts // HG_TM, width), lambda b, h, s: (b * ns + s, h))
    tri, causal_mask, small_masks, big_masks, sel = _hg_constants()

    def whole(a):
        return pl.BlockSpec(a.shape, lambda b, h, s, nd=a.ndim: (0,) * nd)

    return pl.pallas_call(
        functools.partial(_hg_kernel, nblk=ts // HG_T, hps=hps),
        grid=(batch, HEADS // hps, ns),
        in_specs=[
            spec, spec, spec, spec, spec, sum_spec,
            pl.BlockSpec((1, width), lambda b, h, s: (0, h)),
            whole(tri), whole(causal_mask), whole(small_masks), whole(big_masks), whole(sel),
        ],
        out_specs=spec,
        out_shape=jax.ShapeDtypeStruct((batch * seq, D_MODEL), BF16),
        scratch_shapes=[pltpu.VMEM((hps, HEAD_DIM, HEAD_DIM), F32)],
        compiler_params=pltpu.CompilerParams(
            dimension_semantics=("parallel", "parallel", "arbitrary"),
            vmem_limit_bytes=VMEM_LIMIT),
        name="hgrn2",
    )(hq, g_hi, g_lo, hk, hv, g_sum, hg_norm_g2, tri, causal_mask, small_masks, big_masks, sel)


def _out_kernel(x_ref, sbo_ref, hgo_ref, ng_ref, bg_ref, fg_ref,
                wz_sb_ref, wz_hg_ref, wgate_ref, wsb_ref, whg_ref, wout_ref, o_ref):
    x = x_ref[...]
    ms = jnp.mean(x * x, axis=-1, keepdims=True)
    h = (x * lax.rsqrt(ms + RMS_EPS) * ng_ref[...]).astype(BF16)

    sb_z = _dot(h, wz_sb_ref[...])
    a_sb = sbo_ref[...].astype(F32) * (sb_z * _sigmoid(sb_z))
    u_sb = _dot(a_sb.astype(BF16), wsb_ref[...])

    hg_z = _dot(h, wz_hg_ref[...])
    a_hg = hgo_ref[...].astype(F32) * (hg_z * _sigmoid(hg_z))
    u_hg = _dot(a_hg.astype(BF16), whg_ref[...])

    gates = _sigmoid(_dot(h, wgate_ref[...]) + bg_ref[...])
    y = gates[:, :D_MODEL] * u_sb + gates[:, D_MODEL:] * u_hg
    r = x + _dot(y.astype(BF16), wout_ref[...])
    ms2 = jnp.mean(r * r, axis=-1, keepdims=True)
    o_ref[...] = r * lax.rsqrt(ms2 + RMS_EPS) * fg_ref[...]


def _out_call(x2, sb_o, hg_o, norm_g, b_gate, final_g, w_in_bf, w_sb, w_hg, w_out, tm):
    m = x2.shape[0]
    rows = lambda i: (i, 0)
    const = lambda i: (0, 0)

    def wcol(off, width):
        return pl.BlockSpec((D_MODEL, width), lambda i, o=off // width: (0, o))

    return pl.pallas_call(
        _out_kernel,
        grid=(m // tm,),
        in_specs=[
            pl.BlockSpec((tm, D_MODEL), rows),
            pl.BlockSpec((tm, D_MODEL), rows),
            pl.BlockSpec((tm, D_MODEL), rows),
            pl.BlockSpec((1, D_MODEL), const),
            pl.BlockSpec((1, 2 * D_MODEL), const),
            pl.BlockSpec((1, D_MODEL), const),
            wcol(OFF_SB_Z, D_MODEL), wcol(OFF_HG_Z, D_MODEL), wcol(OFF_GATES, 2 * D_MODEL),
            pl.BlockSpec((D_MODEL, D_MODEL), const),
            pl.BlockSpec((D_MODEL, D_MODEL), const),
            pl.BlockSpec((D_MODEL, D_MODEL), const),
        ],
        out_specs=pl.BlockSpec((tm, D_MODEL), rows),
        out_shape=jax.ShapeDtypeStruct((m, D_MODEL), F32),
        compiler_params=pltpu.CompilerParams(
            dimension_semantics=("parallel",), vmem_limit_bytes=VMEM_LIMIT),
        name="out_stage",
    )(x2, sb_o, hg_o, norm_g, b_gate, final_g, w_in_bf, w_in_bf, w_in_bf, w_sb, w_hg, w_out)


def kernel(x, norm_g, w_in, b_gate, lb_logits, hg_norm_g, w_sb_proj, w_hg_proj, w_out, final_norm_g):
    batch, seq, d = x.shape
    assert d == D_MODEL and norm_g.shape[0] == 1, "single-layer block of width 1024"
    x2 = x.reshape(batch * seq, d)
    w_in_bf = w_in[0].astype(BF16)
    lbl2 = lb_logits.reshape(lb_logits.shape[0], HEADS * HEAD_DIM)

    q, k, v, hq, g_hi, g_lo, hk, hv, g_sum = _proj_call(x2, norm_g, lbl2, w_in_bf,
                                                        tm=min(1024, batch * seq), tn=256)
    sb_o = _sb_call(q, k, v, batch, seq, tq=min(256, seq), hps=4)
    hg_o = _hg_call(hq, g_hi, g_lo, hk, hv, g_sum, hg_norm_g.reshape(1, HEADS * HEAD_DIM),
                    batch, seq, ts=min(1024, seq), hps=4)
    out = _out_call(x2, sb_o, hg_o, norm_g, b_gate, final_norm_g.reshape(1, d), w_in_bf,
                    w_sb_proj[0].astype(BF16), w_hg_proj[0].astype(BF16), w_out[0].astype(BF16),
                    tm=256)
    return out.reshape(batch, seq, d)
```

```python
import functools

import numpy as np
import jax
import jax.numpy as jnp
from jax import lax
from jax.experimental import pallas as pl
from jax.experimental.pallas import tpu as pltpu

F32 = jnp.float32
BF16 = jnp.bfloat16

D_MODEL = 1024
HEADS = 8
HEAD_DIM = 128
RMS_EPS = 1e-6
OFF_SB_Q, OFF_SB_K, OFF_SB_V, OFF_SB_Z = 0, 1024, 2048, 3072
OFF_HG_Q, OFF_HG_F, OFF_HG_I, OFF_HG_Z = 4096, 5120, 6144, 7168
OFF_GATES = 8192
LOG2E = 1.4426950408889634
Q_SCALE = HEAD_DIM ** -0.5 * LOG2E

VMEM_LIMIT = 56 * 1024 * 1024


def _sigmoid(x):
    return 1.0 / (1.0 + jnp.exp(-x))


def _dot(a, b):
    return jnp.dot(a, b, preferred_element_type=F32)


def _dot_nt(a, b):
    return lax.dot_general(a, b, (((1,), (1,)), ((), ())), preferred_element_type=F32)


def _dot_tn(a, b):
    return lax.dot_general(a, b, (((0,), (0,)), ((), ())), preferred_element_type=F32)


def _proj_kernel(x_ref, ng_ref, lbl_ref, wq_ref, wk_ref, wv_ref, whq_ref, whf_ref, whi_ref,
                 q_out, k_out, v_out, hq_out, ghi_out, glo_out, hk_out, hv_out, gsum_out, *, nsub):
    sub = x_ref.shape[0] // nsub
    tiles = [slice(t * sub, (t + 1) * sub) for t in range(nsub)]
    hs = []
    for rows in tiles:
        x = x_ref[rows, :]
        ms = jnp.mean(x * x, axis=-1, keepdims=True)
        hs.append((x * lax.rsqrt(ms + RMS_EPS) * ng_ref[...]).astype(BF16))

    lbl = lbl_ref[...]
    e = jnp.exp(lbl - jnp.max(lbl, axis=0, keepdims=True))
    lb = e[0:1, :] / jnp.sum(e, axis=0, keepdims=True)

    sums_per_tile = sub // HG_TM
    for t, rows in enumerate(tiles):
        f = lb + (1.0 - lb) * _sigmoid(_dot(hs[t], whf_ref[...]))
        g2 = jnp.log(f) * LOG2E
        g_hi = g2.astype(BF16)
        ghi_out[rows, :] = g_hi
        glo_out[rows, :] = (g2 - g_hi.astype(F32)).astype(BF16)
        gsum_out[t * sums_per_tile:(t + 1) * sums_per_tile, :] = jnp.sum(
            g2.reshape(sums_per_tile, HG_TM, g2.shape[1]), axis=1)
        hk_out[rows, :] = (1.0 - f).astype(BF16)
    for t, rows in enumerate(tiles):
        hq = _dot(hs[t], whq_ref[...])
        hq_out[rows, :] = (hq * _sigmoid(hq)).astype(BF16)
    for t, rows in enumerate(tiles):
        q_out[rows, :] = (_dot(hs[t], wq_ref[...]) * Q_SCALE).astype(BF16)
    for w_ref, out in ((whi_ref, hv_out), (wk_ref, k_out), (wv_ref, v_out)):
        for t, rows in enumerate(tiles):
            out[rows, :] = _dot(hs[t], w_ref[...]).astype(BF16)


def _proj_call(x2, norm_g, lb_logits2, w_in_bf, tm, nsub):
    m = x2.shape[0]

    def wspec(off):
        return pl.BlockSpec((D_MODEL, D_MODEL), lambda i, o=off // D_MODEL: (0, o))

    out_spec = pl.BlockSpec((tm, D_MODEL), lambda i: (i, 0))
    bf = jax.ShapeDtypeStruct((m, D_MODEL), BF16)
    return pl.pallas_call(
        functools.partial(_proj_kernel, nsub=nsub),
        grid=(m // tm,),
        in_specs=[
            pl.BlockSpec((tm, D_MODEL), lambda i: (i, 0)),
            pl.BlockSpec((1, D_MODEL), lambda i: (0, 0)),
            pl.BlockSpec(lb_logits2.shape, lambda i: (0, 0)),
            wspec(OFF_SB_Q), wspec(OFF_SB_K), wspec(OFF_SB_V),
            wspec(OFF_HG_Q), wspec(OFF_HG_F), wspec(OFF_HG_I),
        ],
        out_specs=[out_spec] * 8 + [pl.BlockSpec((tm // HG_TM, D_MODEL), lambda i: (i, 0))],
        out_shape=[bf] * 8 + [jax.ShapeDtypeStruct((m // HG_TM, D_MODEL), F32)],
        compiler_params=pltpu.CompilerParams(
            dimension_semantics=("parallel",), vmem_limit_bytes=VMEM_LIMIT),
        name="proj",
    )(x2, norm_g, lb_logits2, w_in_bf, w_in_bf, w_in_bf, w_in_bf, w_in_bf, w_in_bf)


SB_DEAD = 160.0
SB_PAST_DEAD = 1e30


def _sb_kernel(q_ref, k_ref, v_ref, tri_ref, o_ref, acc_ref, c_ref, *, tq, hps):
    i = pl.program_id(2)
    tri = tri_ref[...]
    head_lanes = [slice(hh * HEAD_DIM, (hh + 1) * HEAD_DIM) for hh in range(hps)]

    def visit(blocks):
        tiles = [(jb, kind, hh) for jb, kind in blocks for hh in range(hps)]
        row = lax.broadcasted_iota(jnp.int32, (tq, tq), 0)
        col = lax.broadcasted_iota(jnp.int32, (tq, tq), 1)
        causal = col < row

        zs = []
        for jb, kind, hh in tiles:
            kb = k_ref[pl.ds(pl.multiple_of(jb * tq, tq), tq), head_lanes[hh]]
            zs.append(_dot_nt(q_ref[:, head_lanes[hh]], kb))

        parts = []
        for (jb, kind, hh), z in zip(tiles, zs):
            neg = jnp.minimum(z, 0.0)
            pos = z - neg
            l2 = jnp.log2(1.0 + jnp.exp2(neg - pos))
            sp = pos + l2
            if kind == "diag":
                sp = jnp.where(causal, sp, 0.0)
            sr = _dot(sp.astype(BF16), tri)
            parts.append((neg - l2, sr))

        carry = [c_ref[:, lanes] for lanes in head_lanes]
        acc = [acc_ref[:, lanes] for lanes in head_lanes]
        for (jb, kind, hh), (log_beta, sr) in zip(tiles, parts):
            c = carry[hh]
            if kind == "left":
                c = c + jnp.where(i == 0, SB_PAST_DEAD, 0.0)
            c_full = jnp.concatenate([c] * (tq // 128), axis=1)
            w = jnp.exp2(log_beta - sr[:, :tq] - c_full)
            if kind == "diag":
                w = jnp.where(causal, w, 0.0)
            vb = v_ref[pl.ds(pl.multiple_of(jb * tq, tq), tq), head_lanes[hh]]
            acc[hh] = acc[hh] + _dot(w.astype(BF16), vb)
            carry[hh] = c + sr[:, tq:]
        for hh, lanes in enumerate(head_lanes):
            c_ref[:, lanes] = carry[hh]
            acc_ref[:, lanes] = acc[hh]

    def alive():
        return (jnp.min(c_ref[...]) < SB_DEAD).astype(jnp.int32)

    acc_ref[...] = jnp.zeros_like(acc_ref)
    c_ref[...] = jnp.zeros_like(c_ref)
    visit([(i, "diag"), (jnp.maximum(i - 1, 0), "left")])

    def cond(carry):
        jb, live = carry
        return jnp.logical_and(jb >= 0, live != 0)

    def body(carry):
        jb, _ = carry
        visit([(jb, "plain")])
        return jb - 1, alive()

    lax.while_loop(cond, body, (i - 2, alive()))
    o_ref[...] = acc_ref[...].astype(o_ref.dtype)


def _sb_tri(tq):
    k = np.arange(tq)[:, None]
    j = np.arange(tq)[None, :]
    tri = np.concatenate([(k > j), np.ones((tq, 128), bool)], axis=1)
    return jnp.asarray(tri, dtype=BF16)


def _sb_call(q, k, v, batch, seq, tq, hps):
    nq = seq // tq
    width = hps * HEAD_DIM
    return pl.pallas_call(
        functools.partial(_sb_kernel, tq=tq, hps=hps),
        grid=(batch, HEADS // hps, nq),
        in_specs=[
            pl.BlockSpec((tq, width), lambda b, h, i: (b * nq + i, h)),
            pl.BlockSpec((seq, width), lambda b, h, i: (b, h)),
            pl.BlockSpec((seq, width), lambda b, h, i: (b, h)),
            pl.BlockSpec((tq, tq + 128), lambda b, h, i: (0, 0)),
        ],
        out_specs=pl.BlockSpec((tq, width), lambda b, h, i: (b * nq + i, h)),
        out_shape=jax.ShapeDtypeStruct((batch * seq, D_MODEL), BF16),
        scratch_shapes=[pltpu.VMEM((tq, width), F32), pltpu.VMEM((tq, width), F32)],
        compiler_params=pltpu.CompilerParams(
            dimension_semantics=("parallel", "parallel", "arbitrary"),
            vmem_limit_bytes=VMEM_LIMIT),
        name="sb_attn",
    )(q, k, v, _sb_tri(tq))


HG_T = 128
HG_HALF = HG_T // 2
HG_SMALL_LEVELS = (0, 1, 2)
HG_BIG_LEVELS = (3, 4, 5, 6)
HG_TM = 64
HG_MILD_BITS = 100.0
HG_MILD_UNROLL = 8


def _hg_constants():
    t = np.arange(HG_T)[:, None]
    s = np.arange(HG_T)[None, :]
    top = np.floor(np.log2(np.maximum(np.bitwise_xor(t, s), 1))).astype(np.int64)
    small = [t == s] + [(t > s) & (top == lv) for lv in HG_SMALL_LEVELS]
    u = np.arange(HG_HALF)
    big = [(u[:, None] >> lv) == (u[None, :] >> lv) for lv in HG_BIG_LEVELS]
    r = np.arange(HG_T)
    sel = [(r & 1).astype(np.float32)] + [np.where((r >> lv) & 1, 1.0, -1.0) for lv in (1, 2)]
    sel = np.broadcast_to(np.stack(sel)[:, :, None], (3, HG_T, HEAD_DIM))
    return (jnp.asarray(t >= s, dtype=BF16),
            jnp.asarray(t >= s, dtype=F32),
            jnp.asarray(np.stack(small), dtype=F32),
            jnp.asarray(np.stack(big), dtype=F32),
            jnp.asarray(sel, dtype=F32))


def _take_rows(a, starts, size):
    parts = [a[s:s + size, :] for s in starts]
    return parts[0] if len(parts) == 1 else jnp.concatenate(parts, axis=0)


def _small_anchor(cum, level):
    rows, lanes = cum.shape
    parts = []
    if level == 2:
        for s in range(0, rows, 8):
            parts.append(jnp.broadcast_to(cum[s + 3:s + 4, :], (8, lanes)))
    else:
        sub = lax.broadcasted_iota(jnp.int32, (8, lanes), 0)
        for s in range(0, rows, 8):
            lo = jnp.broadcast_to(cum[s + 1:s + 2, :], (8, lanes))
            hi = jnp.broadcast_to(cum[s + 5:s + 6, :], (8, lanes))
            parts.append(jnp.where(sub < 4, lo, hi))
    return jnp.concatenate(parts, axis=0)


def _hg_kernel(q_ref, ghi_ref, glo_ref, k_ref, v_ref, gsum_ref, gain_ref, tri_ref, cmask_ref,
               smask_ref, bmask_ref, sel_ref, o_ref, state_ref, *, nblk, hps):
    @pl.when(pl.program_id(2) == 0)
    def _():
        state_ref[...] = jnp.zeros_like(state_ref)

    tri = tri_ref[...]
    heads = range(hps)
    width = hps * HEAD_DIM

    def load_block(r, t_rows):
        rows = pl.ds(pl.multiple_of(r * t_rows, t_rows), t_rows)
        q_bf, k_bf, v_bf, g_hi, g_lo = [
            [ref[rows, hh * HEAD_DIM:(hh + 1) * HEAD_DIM] for hh in heads]
            for ref in (q_ref, k_ref, v_ref, ghi_ref, glo_ref)]
        tri_t = tri[:t_rows, :t_rows]
        cum2 = [_dot(tri_t, jnp.concatenate([g_hi[h], g_lo[h]], axis=1)) for h in heads]
        cum = [c[:, :HEAD_DIM] + c[:, HEAD_DIM:] for c in cum2]
        return rows, q_bf, k_bf, v_bf, g_hi, g_lo, cum

    def store_block(rows, h, o):
        ms = jnp.mean(o * o, axis=-1, keepdims=True)
        lanes = slice(h * HEAD_DIM, (h + 1) * HEAD_DIM)
        o_ref[rows, lanes] = (o * lax.rsqrt(ms + RMS_EPS) * gain_ref[:, lanes]).astype(o_ref.dtype)

    def mild_body(r, carry):
        blocks = [load_block(r * HG_MILD_UNROLL + j, HG_TM) for j in range(HG_MILD_UNROLL)]
        items = [(j, h) for j in range(HG_MILD_UNROLL) for h in heads]
        causal = cmask_ref[:HG_TM, :HG_TM] > 0.0
        qt_bf, kt, decay = {}, {}, {}
        for j, h in items:
            _, q_bf, k_bf, _, _, _, cum = blocks[j]
            qt_bf[j, h] = (q_bf[h].astype(F32) * jnp.exp2(cum[h])).astype(BF16)
            kt[j, h] = k_bf[h].astype(F32) * jnp.exp2(-cum[h])
            decay[j, h] = jnp.exp2(cum[h][HG_TM - 1:HG_TM, :])
        ps = {it: _dot_nt(qt_bf[it], kt[it].astype(BF16)) for it in items}
        kv = {(j, h): _dot_tn(blocks[j][3][h], (kt[j, h] * decay[j, h]).astype(BF16)) for j, h in items}
        o_intra = {(j, h): _dot(jnp.where(causal, ps[j, h], 0.0).astype(BF16), blocks[j][3][h])
                   for j, h in items}
        states = {}
        for h in heads:
            state = state_ref[h]
            for j in range(HG_MILD_UNROLL):
                states[j, h] = state
                state = state * decay[j, h] + kv[j, h]
            state_ref[h] = state
        o_inter = {it: _dot_nt(qt_bf[it], states[it].astype(BF16)) for it in items}
        for j, h in items:
            store_block(blocks[j][0], h, o_intra[j, h] + o_inter[j, h])
        return carry

    def body(r, carry):
        rows, q_bf, k_bf, v_bf, g_hi, g_lo, cum = load_block(r, HG_T)
        q = [a.astype(F32) for a in q_bf]
        k = [a.astype(F32) for a in k_bf]
        v = [a.astype(F32) for a in v_bf]

        prods = [[(_dot_nt(q_bf[h], k_bf[h]) * smask_ref[0]).astype(BF16)] for h in heads]
        for idx, level in enumerate(HG_SMALL_LEVELS):
            ps = []
            for h in heads:
                if level == 0:
                    d = g_hi[h].astype(F32) + g_lo[h].astype(F32)
                else:
                    d = cum[h] - _small_anchor(cum[h], level)
                e = jnp.exp2(d * sel_ref[idx])
                ps.append(_dot_nt((q[h] * e).astype(BF16), (k[h] * e).astype(BF16)))
            for h in heads:
                prods[h].append((ps[h] * smask_ref[idx + 1]).astype(BF16))

        big = [[] for _ in heads]
        for idx, level in enumerate(HG_BIG_LEVELS):
            half = 1 << level
            starts = list(range(0, HG_T, 2 * half))
            upper = [s + half for s in starts]
            ps = []
            for h in heads:
                anchor = jnp.concatenate(
                    [jnp.broadcast_to(cum[h][s + half - 1:s + half, :], (half, HEAD_DIM))
                     for s in starts], axis=0)
                xu = _take_rows(q[h], upper, half) * jnp.exp2(_take_rows(cum[h], upper, half) - anchor)
                xl = _take_rows(k[h], starts, half) * jnp.exp2(anchor - _take_rows(cum[h], starts, half))
                ps.append(_dot_nt(xu.astype(BF16), xl.astype(BF16)))
            for h in heads:
                big[h].append((ps[h] * bmask_ref[idx]).astype(BF16))

        n_small = 1 + len(HG_SMALL_LEVELS)
        outs, new_states = [], []
        for h in heads:
            state = state_ref[h]
            total = cum[h][HG_T - 1:HG_T, :]
            o = _dot(jnp.concatenate(prods[h], axis=1), jnp.concatenate([v_bf[h]] * n_small, axis=0))
            o = o + _dot_nt((q[h] * jnp.exp2(cum[h])).astype(BF16), state.astype(BF16))
            k_dec = (k[h] * jnp.exp2(total - cum[h])).astype(BF16)
            new_states.append(state * jnp.exp2(total) + _dot_tn(v_bf[h], k_dec))
            outs.append(o)
        contribs = [[_dot(big[h][idx], _take_rows(v[h], range(0, HG_T, 2 << level), 1 << level).astype(BF16))
                     for idx, level in enumerate(HG_BIG_LEVELS)] for h in heads]

        for h in heads:
            state_ref[h] = new_states[h]
            o_tiles = [outs[h][s:s + 8, :] for s in range(0, HG_T, 8)]
            for idx, level in enumerate(HG_BIG_LEVELS):
                half = 1 << level
                tile_ids = [(s + half + r) // 8 for s in range(0, HG_T, 2 * half) for r in range(0, half, 8)]
                for n, tid in enumerate(tile_ids):
                    o_tiles[tid] = o_tiles[tid] + contribs[h][idx][8 * n:8 * n + 8, :]
            store_block(rows, h, jnp.concatenate(o_tiles, axis=0))
        return carry

    mild = jnp.min(gsum_ref[...]) >= -HG_MILD_BITS

    @pl.when(mild)
    def _():
        lax.fori_loop(0, nblk * HG_T // (HG_TM * HG_MILD_UNROLL), mild_body, 0)

    @pl.when(jnp.logical_not(mild))
    def _():
        lax.fori_loop(0, nblk, body, 0)


def _hg_call(hq, g_hi, g_lo, hk, hv, g_sum, hg_norm_g2, batch, seq, ts, hps):
    ns = seq // ts
    width = hps * HEAD_DIM
    spec = pl.BlockSpec((ts, width), lambda b, h, s: (b * ns + s, h))
    sum_spec = pl.BlockSpec((ts // HG_TM, width), lambda b, h, s: (b * ns + s, h))
    tri, causal_mask, small_masks, big_masks, sel = _hg_constants()

    def whole(a):
        return pl.BlockSpec(a.shape, lambda b, h, s, nd=a.ndim: (0,) * nd)

    return pl.pallas_call(
        functools.partial(_hg_kernel, nblk=ts // HG_T, hps=hps),
        grid=(batch, HEADS // hps, ns),
        in_specs=[
            spec, spec, spec, spec, spec, sum_spec,
            pl.BlockSpec((1, width), lambda b, h, s: (0, h)),
            whole(tri), whole(causal_mask), whole(small_masks), whole(big_masks), whole(sel),
        ],
        out_specs=spec,
        out_shape=jax.ShapeDtypeStruct((batch * seq, D_MODEL), BF16),
        scratch_shapes=[pltpu.VMEM((hps, HEAD_DIM, HEAD_DIM), F32)],
        compiler_params=pltpu.CompilerParams(
            dimension_semantics=("parallel", "parallel", "arbitrary"),
            vmem_limit_bytes=VMEM_LIMIT),
        name="hgrn2",
    )(hq, g_hi, g_lo, hk, hv, g_sum, hg_norm_g2, tri, causal_mask, small_masks, big_masks, sel)


def _out_kernel(x_ref, sbo_ref, hgo_ref, ng_ref, bg_ref, fg_ref,
                wz_sb_ref, wz_hg_ref, wgate_ref, wsb_ref, whg_ref, wout_ref, o_ref, *, nsub):
    sub = x_ref.shape[0] // nsub
    tiles = [slice(t * sub, (t + 1) * sub) for t in range(nsub)]
    xs = [x_ref[rows, :] for rows in tiles]
    hs = []
    for x in xs:
        ms = jnp.mean(x * x, axis=-1, keepdims=True)
        hs.append((x * lax.rsqrt(ms + RMS_EPS) * ng_ref[...]).astype(BF16))

    sb_z = [_dot(h, wz_sb_ref[...]) for h in hs]
    hg_z = [_dot(h, wz_hg_ref[...]) for h in hs]
    gate_logits = [_dot(h, wgate_ref[...]) for h in hs]

    a_sb = [(sbo_ref[rows, :].astype(F32) * (z * _sigmoid(z))).astype(BF16) for rows, z in zip(tiles, sb_z)]
    a_hg = [(hgo_ref[rows, :].astype(F32) * (z * _sigmoid(z))).astype(BF16) for rows, z in zip(tiles, hg_z)]
    u_sb = [_dot(a, wsb_ref[...]) for a in a_sb]
    u_hg = [_dot(a, whg_ref[...]) for a in a_hg]

    ys = []
    for t in range(nsub):
        gates = _sigmoid(gate_logits[t] + bg_ref[...])
        ys.append((gates[:, :D_MODEL] * u_sb[t] + gates[:, D_MODEL:] * u_hg[t]).astype(BF16))
    rs = [x + _dot(y, wout_ref[...]) for x, y in zip(xs, ys)]
    for rows, r in zip(tiles, rs):
        ms2 = jnp.mean(r * r, axis=-1, keepdims=True)
        o_ref[rows, :] = r * lax.rsqrt(ms2 + RMS_EPS) * fg_ref[...]


def _out_call(x2, sb_o, hg_o, norm_g, b_gate, final_g, w_in_bf, w_sb, w_hg, w_out, tm, nsub):
    m = x2.shape[0]
    rows = lambda i: (i, 0)
    const = lambda i: (0, 0)

    def wcol(off, width):
        return pl.BlockSpec((D_MODEL, width), lambda i, o=off // width: (0, o))

    return pl.pallas_call(
        functools.partial(_out_kernel, nsub=nsub),
        grid=(m // tm,),
        in_specs=[
            pl.BlockSpec((tm, D_MODEL), rows),
            pl.BlockSpec((tm, D_MODEL), rows),
            pl.BlockSpec((tm, D_MODEL), rows),
            pl.BlockSpec((1, D_MODEL), const),
            pl.BlockSpec((1, 2 * D_MODEL), const),
            pl.BlockSpec((1, D_MODEL), const),
            wcol(OFF_SB_Z, D_MODEL), wcol(OFF_HG_Z, D_MODEL), wcol(OFF_GATES, 2 * D_MODEL),
            pl.BlockSpec((D_MODEL, D_MODEL), const),
            pl.BlockSpec((D_MODEL, D_MODEL), const),
            pl.BlockSpec((D_MODEL, D_MODEL), const),
        ],
        out_specs=pl.BlockSpec((tm, D_MODEL), rows),
        out_shape=jax.ShapeDtypeStruct((m, D_MODEL), F32),
        compiler_params=pltpu.CompilerParams(
            dimension_semantics=("parallel",), vmem_limit_bytes=VMEM_LIMIT),
        name="out_stage",
    )(x2, sb_o, hg_o, norm_g, b_gate, final_g, w_in_bf, w_in_bf, w_in_bf, w_sb, w_hg, w_out)


def kernel(x, norm_g, w_in, b_gate, lb_logits, hg_norm_g, w_sb_proj, w_hg_proj, w_out, final_norm_g):
    batch, seq, d = x.shape
    assert d == D_MODEL and norm_g.shape[0] == 1, "single-layer block of width 1024"
    x2 = x.reshape(batch * seq, d)
    w_in_bf = w_in[0].astype(BF16)
    lbl2 = lb_logits.reshape(lb_logits.shape[0], HEADS * HEAD_DIM)

    q, k, v, hq, g_hi, g_lo, hk, hv, g_sum = _proj_call(x2, norm_g, lbl2, w_in_bf,
                                                        tm=512, nsub=2)
    sb_o = _sb_call(q, k, v, batch, seq, tq=min(256, seq), hps=4)
    hg_o = _hg_call(hq, g_hi, g_lo, hk, hv, g_sum, hg_norm_g.reshape(1, HEADS * HEAD_DIM),
                    batch, seq, ts=min(1024, seq), hps=4)
    out = _out_call(x2, sb_o, hg_o, norm_g, b_gate, final_norm_g.reshape(1, d), w_in_bf,
                    w_sb_proj[0].astype(BF16), w_hg_proj[0].astype(BF16), w_out[0].astype(BF16),
                    tm=512, nsub=4)
    return out.reshape(batch, seq, d)
```

```python
import functools

import numpy as np
import jax
import jax.numpy as jnp
from jax import lax
from jax.experimental import pallas as pl
from jax.experimental.pallas import tpu as pltpu

F32 = jnp.float32
BF16 = jnp.bfloat16

D_MODEL = 1024
HEADS = 8
HEAD_DIM = 128
RMS_EPS = 1e-6
OFF_SB_Q, OFF_SB_K, OFF_SB_V, OFF_SB_Z = 0, 1024, 2048, 3072
OFF_HG_Q, OFF_HG_F, OFF_HG_I, OFF_HG_Z = 4096, 5120, 6144, 7168
OFF_GATES = 8192
LOG2E = 1.4426950408889634
Q_SCALE = HEAD_DIM ** -0.5 * LOG2E

VMEM_LIMIT = 56 * 1024 * 1024


def _sigmoid(x):
    return 1.0 / (1.0 + jnp.exp(-x))


def _dot(a, b):
    return jnp.dot(a, b, preferred_element_type=F32)


def _dot_nt(a, b):
    return lax.dot_general(a, b, (((1,), (1,)), ((), ())), preferred_element_type=F32)


def _dot_tn(a, b):
    return lax.dot_general(a, b, (((0,), (0,)), ((), ())), preferred_element_type=F32)


def _proj_kernel(x_ref, ng_ref, lbl_ref, wq_ref, wk_ref, wv_ref, whq_ref, whf_ref, whi_ref,
                 q_out, k_out, v_out, hq_out, ghi_out, glo_out, hk_out, hv_out, gsum_out, *, nsub):
    sub = x_ref.shape[0] // nsub
    tiles = [slice(t * sub, (t + 1) * sub) for t in range(nsub)]
    hs = []
    for rows in tiles:
        x = x_ref[rows, :]
        ms = jnp.mean(x * x, axis=-1, keepdims=True)
        hs.append((x * lax.rsqrt(ms + RMS_EPS) * ng_ref[...]).astype(BF16))

    lbl = lbl_ref[...]
    e = jnp.exp(lbl - jnp.max(lbl, axis=0, keepdims=True))
    lb = e[0:1, :] / jnp.sum(e, axis=0, keepdims=True)

    sums_per_tile = sub // HG_TM
    for t, rows in enumerate(tiles):
        f = lb + (1.0 - lb) * _sigmoid(_dot(hs[t], whf_ref[...]))
        g2 = jnp.log(f) * LOG2E
        g_hi = g2.astype(BF16)
        ghi_out[rows, :] = g_hi
        glo_out[rows, :] = (g2 - g_hi.astype(F32)).astype(BF16)
        gsum_out[t * sums_per_tile:(t + 1) * sums_per_tile, :] = jnp.sum(
            g2.reshape(sums_per_tile, HG_TM, g2.shape[1]), axis=1)
        hk_out[rows, :] = (1.0 - f).astype(BF16)
    for t, rows in enumerate(tiles):
        hq = _dot(hs[t], whq_ref[...])
        hq_out[rows, :] = (hq * _sigmoid(hq)).astype(BF16)
    for t, rows in enumerate(tiles):
        q_out[rows, :] = (_dot(hs[t], wq_ref[...]) * Q_SCALE).astype(BF16)
    for w_ref, out in ((whi_ref, hv_out), (wk_ref, k_out), (wv_ref, v_out)):
        for t, rows in enumerate(tiles):
            out[rows, :] = _dot(hs[t], w_ref[...]).astype(BF16)


def _proj_call(x2, norm_g, lb_logits2, w_in_bf, tm, nsub):
    m = x2.shape[0]

    def wspec(off):
        return pl.BlockSpec((D_MODEL, D_MODEL), lambda i, o=off // D_MODEL: (0, o))

    out_spec = pl.BlockSpec((tm, D_MODEL), lambda i: (i, 0))
    bf = jax.ShapeDtypeStruct((m, D_MODEL), BF16)
    return pl.pallas_call(
        functools.partial(_proj_kernel, nsub=nsub),
        grid=(m // tm,),
        in_specs=[
            pl.BlockSpec((tm, D_MODEL), lambda i: (i, 0)),
            pl.BlockSpec((1, D_MODEL), lambda i: (0, 0)),
            pl.BlockSpec(lb_logits2.shape, lambda i: (0, 0)),
            wspec(OFF_SB_Q), wspec(OFF_SB_K), wspec(OFF_SB_V),
            wspec(OFF_HG_Q), wspec(OFF_HG_F), wspec(OFF_HG_I),
        ],
        out_specs=[out_spec] * 8 + [pl.BlockSpec((tm // HG_TM, D_MODEL), lambda i: (i, 0))],
        out_shape=[bf] * 8 + [jax.ShapeDtypeStruct((m // HG_TM, D_MODEL), F32)],
        compiler_params=pltpu.CompilerParams(
            dimension_semantics=("parallel",), vmem_limit_bytes=VMEM_LIMIT),
        name="proj",
    )(x2, norm_g, lb_logits2, w_in_bf, w_in_bf, w_in_bf, w_in_bf, w_in_bf, w_in_bf)


SB_DEAD = 160.0
SB_PAST_DEAD = 1e30


def _sb_kernel(q_ref, k_ref, v_ref, tri_ref, tri_half_ref, o_ref, acc_ref, c_ref, *, tq, hps, nsub):
    tri = tri_ref[...]
    tri_half = tri_half_ref[...]
    half = tq // 2
    head_lanes = [slice(hh * HEAD_DIM, (hh + 1) * HEAD_DIM) for hh in range(hps)]

    def query_tile(sub, unused):
        i = pl.program_id(2) * nsub + sub
        q0 = pl.multiple_of(sub * tq, tq)

        def logits(rows, kstart, kw, lanes):
            kb = k_ref[pl.ds(pl.multiple_of(kstart, kw), kw), lanes]
            qb = q_ref[pl.ds(q0 + rows.start, rows.stop - rows.start), lanes]
            return _dot_nt(qb, kb)

        def softplus_stage(z, mask):
            kw = z.shape[1]
            neg = jnp.minimum(z, 0.0)
            pos = z - neg
            l2 = jnp.log2(1.0 + jnp.exp2(neg - pos))
            sp = pos + l2
            if mask is not None:
                sp = jnp.where(mask, sp, 0.0)
            return neg - l2, _dot(sp.astype(BF16), tri if kw == tq else tri_half)

        def weights(log_beta, sr, c, mask):
            kw = log_beta.shape[1]
            w = log_beta - sr[:, :kw]
            if c is not None:
                w = w - jnp.concatenate([c] * (kw // 128), axis=1)
            w = jnp.exp2(w)
            if mask is not None:
                w = jnp.where(mask, w, 0.0)
            return w.astype(BF16)

        def values(kstart, kw, lanes):
            return v_ref[pl.ds(pl.multiple_of(kstart, kw), kw), lanes]

        mask_top = (lax.broadcasted_iota(jnp.int32, (half, half), 1)
                    < lax.broadcasted_iota(jnp.int32, (half, half), 0))
        mask_bot = (lax.broadcasted_iota(jnp.int32, (half, tq), 1)
                    < lax.broadcasted_iota(jnp.int32, (half, tq), 0) + half)
        d0 = i * tq
        l0 = jnp.maximum(i - 1, 0) * tq
        top, bot, full = slice(0, half), slice(half, tq), slice(0, tq)
        z_top = [logits(top, d0, half, lanes) for lanes in head_lanes]
        z_bot = [logits(bot, d0, tq, lanes) for lanes in head_lanes]
        z_left = [logits(full, l0, tq, lanes) for lanes in head_lanes]
        s_top = [softplus_stage(z, mask_top) for z in z_top]
        s_bot = [softplus_stage(z, mask_bot) for z in z_bot]
        s_left = [softplus_stage(z, None) for z in z_left]
        for hh, lanes in enumerate(head_lanes):
            pv_top = _dot(weights(*s_top[hh], None, mask_top), values(d0, half, lanes))
            pv_bot = _dot(weights(*s_bot[hh], None, mask_bot), values(d0, tq, lanes))
            c = jnp.concatenate([s_top[hh][1][:, half:], s_bot[hh][1][:, tq:]], axis=0)
            c = c + jnp.where(i == 0, SB_PAST_DEAD, 0.0)
            pv_left = _dot(weights(*s_left[hh], c, None), values(l0, tq, lanes))
            acc_ref[:, lanes] = jnp.concatenate([pv_top, pv_bot], axis=0) + pv_left
            c_ref[:, lanes] = c + s_left[hh][1][:, tq:]

        def alive():
            return (jnp.min(c_ref[...]) < SB_DEAD).astype(jnp.int32)

        def cond(carry):
            jb, live = carry
            return jnp.logical_and(jb >= 0, live != 0)

        def body(carry):
            jb, _ = carry
            zs = [logits(full, jb * tq, tq, lanes) for lanes in head_lanes]
            parts = [softplus_stage(z, None) for z in zs]
            for hh, lanes in enumerate(head_lanes):
                c = c_ref[:, lanes]
                acc_ref[:, lanes] += _dot(weights(*parts[hh], c, None), values(jb * tq, tq, lanes))
                c_ref[:, lanes] = c + parts[hh][1][:, tq:]
            return jb - 1, alive()

        lax.while_loop(cond, body, (i - 2, alive()))
        o_ref[pl.ds(q0, tq), :] = acc_ref[...].astype(o_ref.dtype)
        return unused

    lax.fori_loop(0, nsub, query_tile, 0)


def _sb_tri(n):
    k = np.arange(n)[:, None]
    j = np.arange(n)[None, :]
    tri = np.concatenate([(k > j), np.ones((n, 128), bool)], axis=1)
    return jnp.asarray(tri, dtype=BF16)


def _sb_call(q, k, v, batch, seq, tq, hps, nsub):
    nq = seq // (tq * nsub)
    width = hps * HEAD_DIM
    return pl.pallas_call(
        functools.partial(_sb_kernel, tq=tq, hps=hps, nsub=nsub),
        grid=(batch, HEADS // hps, nq),
        in_specs=[
            pl.BlockSpec((tq * nsub, width), lambda b, h, i: (b * nq + i, h)),
            pl.BlockSpec((seq, width), lambda b, h, i: (b, h)),
            pl.BlockSpec((seq, width), lambda b, h, i: (b, h)),
            pl.BlockSpec((tq, tq + 128), lambda b, h, i: (0, 0)),
            pl.BlockSpec((tq // 2, tq // 2 + 128), lambda b, h, i: (0, 0)),
        ],
        out_specs=pl.BlockSpec((tq * nsub, width), lambda b, h, i: (b * nq + i, h)),
        out_shape=jax.ShapeDtypeStruct((batch * seq, D_MODEL), BF16),
        scratch_shapes=[pltpu.VMEM((tq, width), F32), pltpu.VMEM((tq, width), F32)],
        compiler_params=pltpu.CompilerParams(
            dimension_semantics=("parallel", "parallel", "arbitrary"),
            vmem_limit_bytes=VMEM_LIMIT),
        name="sb_attn",
    )(q, k, v, _sb_tri(tq), _sb_tri(tq // 2))


HG_T = 128
HG_HALF = HG_T // 2
HG_SMALL_LEVELS = (0, 1, 2)
HG_BIG_LEVELS = (3, 4, 5, 6)
HG_TM = 64
HG_MILD_BITS = 100.0
HG_MILD_UNROLL = 8


def _hg_constants():
    t = np.arange(HG_T)[:, None]
    s = np.arange(HG_T)[None, :]
    top = np.floor(np.log2(np.maximum(np.bitwise_xor(t, s), 1))).astype(np.int64)
    small = [t == s] + [(t > s) & (top == lv) for lv in HG_SMALL_LEVELS]
    u = np.arange(HG_HALF)
    big = [(u[:, None] >> lv) == (u[None, :] >> lv) for lv in HG_BIG_LEVELS]
    r = np.arange(HG_T)
    sel = [(r & 1).astype(np.float32)] + [np.where((r >> lv) & 1, 1.0, -1.0) for lv in (1, 2)]
    sel = np.broadcast_to(np.stack(sel)[:, :, None], (3, HG_T, HEAD_DIM))
    return (jnp.asarray(t >= s, dtype=BF16),
            jnp.asarray(t >= s, dtype=F32),
            jnp.asarray(np.stack(small), dtype=F32),
            jnp.asarray(np.stack(big), dtype=F32),
            jnp.asarray(sel, dtype=F32))


def _take_rows(a, starts, size):
    parts = [a[s:s + size, :] for s in starts]
    return parts[0] if len(parts) == 1 else jnp.concatenate(parts, axis=0)


def _small_anchor(cum, level):
    rows, lanes = cum.shape
    parts = []
    if level == 2:
        for s in range(0, rows, 8):
            parts.append(jnp.broadcast_to(cum[s + 3:s + 4, :], (8, lanes)))
    else:
        sub = lax.broadcasted_iota(jnp.int32, (8, lanes), 0)
        for s in range(0, rows, 8):
            lo = jnp.broadcast_to(cum[s + 1:s + 2, :], (8, lanes))
            hi = jnp.broadcast_to(cum[s + 5:s + 6, :], (8, lanes))
            parts.append(jnp.where(sub < 4, lo, hi))
    return jnp.concatenate(parts, axis=0)


def _hg_kernel(q_ref, ghi_ref, glo_ref, k_ref, v_ref, gsum_ref, gain_ref, tri_ref, cmask_ref,
               smask_ref, bmask_ref, sel_ref, o_ref, state_ref, *, nblk, hps):
    @pl.when(pl.program_id(2) == 0)
    def _():
        state_ref[...] = jnp.zeros_like(state_ref)

    tri = tri_ref[...]
    heads = range(hps)
    width = hps * HEAD_DIM

    def load_block(r, t_rows):
        rows = pl.ds(pl.multiple_of(r * t_rows, t_rows), t_rows)
        q_bf, k_bf, v_bf, g_hi, g_lo = [
            [ref[rows, hh * HEAD_DIM:(hh + 1) * HEAD_DIM] for hh in heads]
            for ref in (q_ref, k_ref, v_ref, ghi_ref, glo_ref)]
        tri_t = tri[:t_rows, :t_rows]
        cum2 = [_dot(tri_t, jnp.concatenate([g_hi[h], g_lo[h]], axis=1)) for h in heads]
        cum = [c[:, :HEAD_DIM] + c[:, HEAD_DIM:] for c in cum2]
        return rows, q_bf, k_bf, v_bf, g_hi, g_lo, cum

    def store_block(rows, h, o):
        ms = jnp.mean(o * o, axis=-1, keepdims=True)
        lanes = slice(h * HEAD_DIM, (h + 1) * HEAD_DIM)
        o_ref[rows, lanes] = (o * lax.rsqrt(ms + RMS_EPS) * gain_ref[:, lanes]).astype(o_ref.dtype)

    def mild_body(r, carry):
        blocks = [load_block(r * HG_MILD_UNROLL + j, HG_TM) for j in range(HG_MILD_UNROLL)]
        items = [(j, h) for j in range(HG_MILD_UNROLL) for h in heads]
        causal = cmask_ref[:HG_TM, :HG_TM] > 0.0
        qt_bf, kt, decay = {}, {}, {}
        for j, h in items:
            _, q_bf, k_bf, _, _, _, cum = blocks[j]
            qt_bf[j, h] = (q_bf[h].astype(F32) * jnp.exp2(cum[h])).astype(BF16)
            kt[j, h] = k_bf[h].astype(F32) * jnp.exp2(-cum[h])
            decay[j, h] = jnp.exp2(cum[h][HG_TM - 1:HG_TM, :])
        ps = {it: _dot_nt(qt_bf[it], kt[it].astype(BF16)) for it in items}
        kv = {(j, h): _dot_tn(blocks[j][3][h], (kt[j, h] * decay[j, h]).astype(BF16)) for j, h in items}
        o_intra = {(j, h): _dot(jnp.where(causal, ps[j, h], 0.0).astype(BF16), blocks[j][3][h])
                   for j, h in items}
        states = {}
        for h in heads:
            state = state_ref[h]
            for j in range(HG_MILD_UNROLL):
                states[j, h] = state
                state = state * decay[j, h] + kv[j, h]
            state_ref[h] = state
        o_inter = {it: _dot_nt(qt_bf[it], states[it].astype(BF16)) for it in items}
        for j, h in items:
            store_block(blocks[j][0], h, o_intra[j, h] + o_inter[j, h])
        return carry

    def body(r, carry):
        rows, q_bf, k_bf, v_bf, g_hi, g_lo, cum = load_block(r, HG_T)
        q = [a.astype(F32) for a in q_bf]
        k = [a.astype(F32) for a in k_bf]
        v = [a.astype(F32) for a in v_bf]

        prods = [[(_dot_nt(q_bf[h], k_bf[h]) * smask_ref[0]).astype(BF16)] for h in heads]
        for idx, level in enumerate(HG_SMALL_LEVELS):
            ps = []
            for h in heads:
                if level == 0:
                    d = g_hi[h].astype(F32) + g_lo[h].astype(F32)
                else:
                    d = cum[h] - _small_anchor(cum[h], level)
                e = jnp.exp2(d * sel_ref[idx])
                ps.append(_dot_nt((q[h] * e).astype(BF16), (k[h] * e).astype(BF16)))
            for h in heads:
                prods[h].append((ps[h] * smask_ref[idx + 1]).astype(BF16))

        big = [[] for _ in heads]
        for idx, level in enumerate(HG_BIG_LEVELS):
            half = 1 << level
            starts = list(range(0, HG_T, 2 * half))
            upper = [s + half for s in starts]
            ps = []
            for h in heads:
                anchor = jnp.concatenate(
                    [jnp.broadcast_to(cum[h][s + half - 1:s + half, :], (half, HEAD_DIM))
                     for s in starts], axis=0)
                xu = _take_rows(q[h], upper, half) * jnp.exp2(_take_rows(cum[h], upper, half) - anchor)
                xl = _take_rows(k[h], starts, half) * jnp.exp2(anchor - _take_rows(cum[h], starts, half))
                ps.append(_dot_nt(xu.astype(BF16), xl.astype(BF16)))
            for h in heads:
                big[h].append((ps[h] * bmask_ref[idx]).astype(BF16))

        n_small = 1 + len(HG_SMALL_LEVELS)
        outs, new_states = [], []
        for h in heads:
            state = state_ref[h]
            total = cum[h][HG_T - 1:HG_T, :]
            o = _dot(jnp.concatenate(prods[h], axis=1), jnp.concatenate([v_bf[h]] * n_small, axis=0))
            o = o + _dot_nt((q[h] * jnp.exp2(cum[h])).astype(BF16), state.astype(BF16))
            k_dec = (k[h] * jnp.exp2(total - cum[h])).astype(BF16)
            new_states.append(state * jnp.exp2(total) + _dot_tn(v_bf[h], k_dec))
            outs.append(o)
        contribs = [[_dot(big[h][idx], _take_rows(v[h], range(0, HG_T, 2 << level), 1 << level).astype(BF16))
                     for idx, level in enumerate(HG_BIG_LEVELS)] for h in heads]

        for h in heads:
            state_ref[h] = new_states[h]
            o_tiles = [outs[h][s:s + 8, :] for s in range(0, HG_T, 8)]
            for idx, level in enumerate(HG_BIG_LEVELS):
                half = 1 << level
                tile_ids = [(s + half + r) // 8 for s in range(0, HG_T, 2 * half) for r in range(0, half, 8)]
                for n, tid in enumerate(tile_ids):
                    o_tiles[tid] = o_tiles[tid] + contribs[h][idx][8 * n:8 * n + 8, :]
            store_block(rows, h, jnp.concatenate(o_tiles, axis=0))
        return carry

    mild = jnp.min(gsum_ref[...]) >= -HG_MILD_BITS

    @pl.when(mild)
    def _():
        lax.fori_loop(0, nblk * HG_T // (HG_TM * HG_MILD_UNROLL), mild_body, 0)

    @pl.when(jnp.logical_not(mild))
    def _():
        lax.fori_loop(0, nblk, body, 0)


def _hg_call(hq, g_hi, g_lo, hk, hv, g_sum, hg_norm_g2, batch, seq, ts, hps):
    ns = seq // ts
    width = hps * HEAD_DIM
    spec = pl.BlockSpec((ts, width), lambda b, h, s: (b * ns + s, h))
    sum_spec = pl.BlockSpec((ts // HG_TM, width), lambda b, h, s: (b * ns + s, h))
    tri, causal_mask, small_masks, big_masks, sel = _hg_constants()

    def whole(a):
        return pl.BlockSpec(a.shape, lambda b, h, s, nd=a.ndim: (0,) * nd)

    return pl.pallas_call(
        functools.partial(_hg_kernel, nblk=ts // HG_T, hps=hps),
        grid=(batch, HEADS // hps, ns),
        in_specs=[
            spec, spec, spec, spec, spec, sum_spec,
            pl.BlockSpec((1, width), lambda b, h, s: (0, h)),
            whole(tri), whole(causal_mask), whole(small_masks), whole(big_masks), whole(sel),
        ],
        out_specs=spec,
        out_shape=jax.ShapeDtypeStruct((batch * seq, D_MODEL), BF16),
        scratch_shapes=[pltpu.VMEM((hps, HEAD_DIM, HEAD_DIM), F32)],
        compiler_params=pltpu.CompilerParams(
            dimension_semantics=("parallel", "parallel", "arbitrary"),
            vmem_limit_bytes=VMEM_LIMIT),
        name="hgrn2",
    )(hq, g_hi, g_lo, hk, hv, g_sum, hg_norm_g2, tri, causal_mask, small_masks, big_masks, sel)


def _out_kernel(x_ref, sbo_ref, hgo_ref, ng_ref, bg_ref, fg_ref,
                wz_sb_ref, wz_hg_ref, wgate_ref, wsb_ref, whg_ref, wout_ref, o_ref, *, nsub):
    sub = x_ref.shape[0] // nsub
    tiles = [slice(t * sub, (t + 1) * sub) for t in range(nsub)]
    xs = [x_ref[rows, :] for rows in tiles]
    hs = []
    for x in xs:
        ms = jnp.mean(x * x, axis=-1, keepdims=True)
        hs.append((x * lax.rsqrt(ms + RMS_EPS) * ng_ref[...]).astype(BF16))

    sb_z = [_dot(h, wz_sb_ref[...]) for h in hs]
    hg_z = [_dot(h, wz_hg_ref[...]) for h in hs]
    gate_logits = [_dot(h, wgate_ref[...]) for h in hs]

    a_sb = [(sbo_ref[rows, :].astype(F32) * (z * _sigmoid(z))).astype(BF16) for rows, z in zip(tiles, sb_z)]
    a_hg = [(hgo_ref[rows, :].astype(F32) * (z * _sigmoid(z))).astype(BF16) for rows, z in zip(tiles, hg_z)]
    u_sb = [_dot(a, wsb_ref[...]) for a in a_sb]
    u_hg = [_dot(a, whg_ref[...]) for a in a_hg]

    ys = []
    for t in range(nsub):
        gates = _sigmoid(gate_logits[t] + bg_ref[...])
        ys.append((gates[:, :D_MODEL] * u_sb[t] + gates[:, D_MODEL:] * u_hg[t]).astype(BF16))
    rs = [x + _dot(y, wout_ref[...]) for x, y in zip(xs, ys)]
    for rows, r in zip(tiles, rs):
        ms2 = jnp.mean(r * r, axis=-1, keepdims=True)
        o_ref[rows, :] = r * lax.rsqrt(ms2 + RMS_EPS) * fg_ref[...]


def _out_call(x2, sb_o, hg_o, norm_g, b_gate, final_g, w_in_bf, w_sb, w_hg, w_out, tm, nsub):
    m = x2.shape[0]
    rows = lambda i: (i, 0)
    const = lambda i: (0, 0)

    def wcol(off, width):
        return pl.BlockSpec((D_MODEL, width), lambda i, o=off // width: (0, o))

    return pl.pallas_call(
        functools.partial(_out_kernel, nsub=nsub),
        grid=(m // tm,),
        in_specs=[
            pl.BlockSpec((tm, D_MODEL), rows),
            pl.BlockSpec((tm, D_MODEL), rows),
            pl.BlockSpec((tm, D_MODEL), rows),
            pl.BlockSpec((1, D_MODEL), const),
            pl.BlockSpec((1, 2 * D_MODEL), const),
            pl.BlockSpec((1, D_MODEL), const),
            wcol(OFF_SB_Z, D_MODEL), wcol(OFF_HG_Z, D_MODEL), wcol(OFF_GATES, 2 * D_MODEL),
            pl.BlockSpec((D_MODEL, D_MODEL), const),
            pl.BlockSpec((D_MODEL, D_MODEL), const),
            pl.BlockSpec((D_MODEL, D_MODEL), const),
        ],
        out_specs=pl.BlockSpec((tm, D_MODEL), rows),
        out_shape=jax.ShapeDtypeStruct((m, D_MODEL), F32),
        compiler_params=pltpu.CompilerParams(
            dimension_semantics=("parallel",), vmem_limit_bytes=VMEM_LIMIT),
        name="out_stage",
    )(x2, sb_o, hg_o, norm_g, b_gate, final_g, w_in_bf, w_in_bf, w_in_bf, w_sb, w_hg, w_out)


def kernel(x, norm_g, w_in, b_gate, lb_logits, hg_norm_g, w_sb_proj, w_hg_proj, w_out, final_norm_g):
    batch, seq, d = x.shape
    assert d == D_MODEL and norm_g.shape[0] == 1, "single-layer block of width 1024"
    x2 = x.reshape(batch * seq, d)
    w_in_bf = w_in[0].astype(BF16)
    lbl2 = lb_logits.reshape(lb_logits.shape[0], HEADS * HEAD_DIM)

    q, k, v, hq, g_hi, g_lo, hk, hv, g_sum = _proj_call(x2, norm_g, lbl2, w_in_bf,
                                                        tm=512, nsub=2)
    sb_o = _sb_call(q, k, v, batch, seq, tq=256, hps=4, nsub=2)
    hg_o = _hg_call(hq, g_hi, g_lo, hk, hv, g_sum, hg_norm_g.reshape(1, HEADS * HEAD_DIM),
                    batch, seq, ts=min(2048, seq), hps=4)
    out = _out_call(x2, sb_o, hg_o, norm_g, b_gate, final_norm_g.reshape(1, d), w_in_bf,
                    w_sb_proj[0].astype(BF16), w_hg_proj[0].astype(BF16), w_out[0].astype(BF16),
                    tm=512, nsub=4)
    return out.reshape(batch, seq, d)
```
